```python
import math
import jax, jax.numpy as jnp
from jax import lax
import numpy as np

D_MODEL = 4096
BATCH = 4
SEQ = 2048
DEPTH = 2

POOL_WINDOWS = (2, 4, 8, 16)
N_POOL_GROUPS = len(POOL_WINDOWS)
POOL_WIDTH = D_MODEL // 2
POOL_GROUP = POOL_WIDTH // N_POOL_GROUPS
HEAD_DIM = 64
N_Q_HEADS = (D_MODEL // 2) // HEAD_DIM
N_KV_HEADS = N_Q_HEADS // 8
Q_PER_KV = N_Q_HEADS // N_KV_HEADS
Q_WIDTH = N_Q_HEADS * HEAD_DIM
KV_WIDTH = N_KV_HEADS * HEAD_DIM
WINDOW = 128
BLOCK = 128
N_BUCKETS = 32
MAX_DISTANCE = 128
IN_WIDTH = POOL_WIDTH + Q_WIDTH + 2 * KV_WIDTH
D_FF = -(-(8 * D_MODEL) // (3 * 256)) * 256
N_MOD = 6
EPS = 1e-6

kernel_name = "hybrid_pool_swa_sink_gated_block"


def rmsnorm(x, gain):
    xf = x.astype(jnp.float32)
    xf = xf * lax.rsqrt(jnp.mean(xf * xf, axis=-1, keepdims=True) + EPS)
    return (xf * gain.astype(jnp.float32)).astype(x.dtype)


def t5_causal_bucket(dist):
    max_exact = N_BUCKETS // 2
    d = jnp.maximum(dist, 0)
    log_ratio = jnp.log(jnp.maximum(d, 1).astype(jnp.float32) / max_exact) / math.log(MAX_DISTANCE / max_exact)
    large = max_exact + (log_ratio * (N_BUCKETS - max_exact)).astype(jnp.int32)
    large = jnp.minimum(large, N_BUCKETS - 1)
    return jnp.where(d < max_exact, d, large)


def band_geometry():
    i = jnp.arange(BLOCK)[:, None]
    j = jnp.arange(2 * BLOCK)[None, :]
    dist = i + BLOCK - j
    in_window = (dist >= 0) & (dist < WINDOW)
    return dist, in_window


def multiscale_pool(u, w_mix, scale):
    B, S, _ = u.shape
    ug = u.astype(jnp.float32).reshape(B, S, N_POOL_GROUPS, POOL_GROUP)
    cs = jnp.pad(jnp.cumsum(ug, axis=1), ((0, 0), (1, 0), (0, 0), (0, 0)))
    t = jnp.arange(S)[:, None]
    win = jnp.array(POOL_WINDOWS, dtype=jnp.int32)[None, :]
    lo = jnp.maximum(t + 1 - win, 0)
    g_idx = jnp.arange(N_POOL_GROUPS)[None, :]
    window_sum = cs[:, 1:] - cs[:, lo, g_idx]
    count = (t + 1 - lo).astype(jnp.float32)
    pooled = (window_sum / count[None, :, :, None] - ug).astype(u.dtype)
    y = jnp.einsum('bsgc,gce->bsge', pooled, w_mix)
    return y.reshape(B, S, POOL_WIDTH) * scale


def sliding_window_sink_attention(q, k, v, sinks, band_bias):
    B, S, _ = q.shape
    nb = S // BLOCK
    qb = q.reshape(B, nb, BLOCK, N_KV_HEADS, Q_PER_KV, HEAD_DIM)

    def band(t):
        tb = t.reshape(B, nb, BLOCK, N_KV_HEADS, HEAD_DIM)
        prev = jnp.pad(tb[:, :-1], ((0, 0), (1, 0), (0, 0), (0, 0), (0, 0)))
        return jnp.concatenate([prev, tb], axis=2)

    kb, vb = band(k), band(v)
    logits = jnp.einsum('bnqhgd,bnkhd->bhgnqk', qb, kb,
                        preferred_element_type=jnp.float32) * (HEAD_DIM ** -0.5)
    logits = logits + band_bias.astype(jnp.float32).reshape(N_KV_HEADS, Q_PER_KV, 1, BLOCK, 2 * BLOCK)
    _, in_window = band_geometry()
    key_pos = jnp.arange(nb)[:, None, None] * BLOCK - BLOCK + jnp.arange(2 * BLOCK)[None, None, :]
    mask = in_window[None] & (key_pos >= 0)
    logits = jnp.where(mask, logits, -jnp.inf)
    sink = jnp.broadcast_to(sinks.astype(jnp.float32).reshape(1, N_KV_HEADS, Q_PER_KV, 1, 1, 1),
                            logits.shape[:-1] + (1,))
    probs = jax.nn.softmax(jnp.concatenate([logits, sink], axis=-1), axis=-1)[..., :-1]
    out = jnp.einsum('bhgnqk,bnkhd->bnqhgd', probs.astype(v.dtype), vb)
    return out.reshape(B, S, Q_WIDTH)


def setup_inputs(seed: int = 0) -> dict:
    key = jax.random.key(seed)
    ks = jax.random.split(key, 20)
    f32 = jnp.float32

    def normal(k, shape, std=1.0):
        return jax.random.normal(k, shape, f32) * std

    def dense(k, shape, fan_in, gain=1.0):
        return normal(k, shape, gain * fan_in ** -0.5)

    return {
        "x": normal(ks[0], (BATCH, SEQ, D_MODEL)),
        "c": normal(ks[1], (BATCH, D_MODEL)),
        "w_ada": dense(ks[2], (DEPTH, D_MODEL, N_MOD * D_MODEL), D_MODEL, 0.5),
        "b_ada": normal(ks[3], (DEPTH, N_MOD * D_MODEL), 0.02),
        "norm1": 1.0 + normal(ks[4], (DEPTH, D_MODEL), 0.05),
        "w_in": dense(ks[5], (DEPTH, D_MODEL, IN_WIDTH), D_MODEL),
        "w_pool_mix": dense(ks[6], (DEPTH, N_POOL_GROUPS, POOL_GROUP, POOL_GROUP), POOL_GROUP),
        "pool_scale": 1.0 + normal(ks[7], (DEPTH, POOL_WIDTH), 0.1),
        "sinks": normal(ks[8], (DEPTH, N_Q_HEADS), 0.5),
        "rel_bias": normal(ks[9], (N_BUCKETS, N_Q_HEADS), 0.5),
        "w_branch_pool": dense(ks[10], (DEPTH, POOL_WIDTH, D_MODEL), POOL_WIDTH),
        "w_branch_attn": dense(ks[11], (DEPTH, Q_WIDTH, D_MODEL), Q_WIDTH),
        "w_gate": dense(ks[12], (DEPTH, D_MODEL, 2 * D_MODEL), D_MODEL),
        "w_out": dense(ks[13], (DEPTH, D_MODEL, D_MODEL), D_MODEL),
        "norm2": 1.0 + normal(ks[14], (DEPTH, D_MODEL), 0.05),
        "w_ffn_in": dense(ks[15], (DEPTH, D_MODEL, 2 * D_FF), D_MODEL),
        "w_ffn_out": dense(ks[16], (DEPTH, D_FF, D_MODEL), D_FF),
        "final_norm": 1.0 + normal(ks[17], (D_MODEL,), 0.05),
    }


def reference(x, c, w_ada, b_ada, norm1, w_in, w_pool_mix, pool_scale, sinks, rel_bias,
              w_branch_pool, w_branch_attn, w_gate, w_out, norm2, w_ffn_in, w_ffn_out,
              final_norm):
    dist, _ = band_geometry()
    band_bias = jnp.transpose(rel_bias[t5_causal_bucket(dist)], (2, 0, 1))
    silu_c = jax.nn.silu(c)
    split_at = [POOL_WIDTH, POOL_WIDTH + Q_WIDTH, POOL_WIDTH + Q_WIDTH + KV_WIDTH]

    for l in range(DEPTH):
        mod = (silu_c @ w_ada[l] + b_ada[l])[:, None, :]
        sh1, sc1, g1, sh2, sc2, g2 = jnp.split(mod, N_MOD, axis=-1)

        h = rmsnorm(x, norm1[l]) * (1 + sc1) + sh1
        u_pool, q, k, v = jnp.split(h @ w_in[l], split_at, axis=-1)
        y_pool = multiscale_pool(u_pool, w_pool_mix[l], pool_scale[l]) @ w_branch_pool[l]
        y_attn = sliding_window_sink_attention(q, k, v, sinks[l], band_bias) @ w_branch_attn[l]
        g_pool, g_attn = jnp.split(jax.nn.sigmoid(h @ w_gate[l]), 2, axis=-1)
        x = x + g1 * ((g_pool * y_pool + g_attn * y_attn) @ w_out[l])

        h2 = rmsnorm(x, norm2[l]) * (1 + sc2) + sh2
        a, b = jnp.split(h2 @ w_ffn_in[l], 2, axis=-1)
        x = x + g2 * ((jax.nn.silu(a) * b) @ w_ffn_out[l])

    return rmsnorm(x, final_norm)
```

```python
import functools
import math

import jax
import jax.numpy as jnp
from jax import lax
from jax.experimental import pallas as pl
from jax.experimental.pallas import tpu as pltpu

D_MODEL = 4096
BATCH = 4
SEQ = 2048
ROWS = BATCH * SEQ
DEPTH = 2
POOL_WINDOWS = (2, 4, 8, 16)
N_GROUPS = len(POOL_WINDOWS)
POOL_WIDTH = D_MODEL // 2
POOL_GROUP = POOL_WIDTH // N_GROUPS
HEAD_DIM = 64
N_Q_HEADS = (D_MODEL // 2) // HEAD_DIM
N_KV_HEADS = N_Q_HEADS // 8
Q_PER_KV = N_Q_HEADS // N_KV_HEADS
Q_WIDTH = N_Q_HEADS * HEAD_DIM
KV_WIDTH = N_KV_HEADS * HEAD_DIM
WINDOW = 128
BLOCK = 128
N_BLOCKS = SEQ // BLOCK
N_BUCKETS = 32
MAX_DISTANCE = 128
D_FF = -(-(8 * D_MODEL) // (3 * 256)) * 256
N_MOD = 6
EPS = 1e-6

V7X_LANES = 128
V7X_VMEM_BYTES = 64 * 1024 * 1024
VMEM_LIMIT = V7X_VMEM_BYTES - 8 * 1024 * 1024

FF_TILE = 512
D_FF_PAD = -(-D_FF // (2 * FF_TILE)) * (2 * FF_TILE)
NEG = -1e30

BF16 = jnp.bfloat16
F32 = jnp.float32


def _params(*sem):
    return pltpu.CompilerParams(dimension_semantics=sem, vmem_limit_bytes=VMEM_LIMIT)


def _dot(a, b):
    return jnp.dot(a, b, preferred_element_type=F32)


def _ada_kernel(c_ref, w_ref, b_ref, o_ref):
    c = c_ref[...]
    s = (c * jax.nn.sigmoid(c)).astype(BF16)
    o_ref[0] = _dot(s, w_ref[0].astype(BF16)) + b_ref[0]


def _ada(c_pad, w_ada, b_ada):
    tn = 512
    n = N_MOD * D_MODEL
    return pl.pallas_call(
        _ada_kernel,
        grid=(DEPTH, n // tn),
        in_specs=[
            pl.BlockSpec((8, D_MODEL), lambda l, j: (0, 0)),
            pl.BlockSpec((1, D_MODEL, tn), lambda l, j: (l, 0, j)),
            pl.BlockSpec((1, 1, tn), lambda l, j: (l, 0, j)),
        ],
        out_specs=pl.BlockSpec((1, 8, tn), lambda l, j: (l, 0, j)),
        out_shape=jax.ShapeDtypeStruct((DEPTH, 8, n), F32),
        compiler_params=_params("arbitrary", "arbitrary"),
        name="ada_mod",
    )(c_pad, w_ada, b_ada.reshape(DEPTH, 1, n))


def _norm_kernel(x_ref, g_ref, sc_ref, sh_ref, o_ref):
    x = x_ref[...]
    r = lax.rsqrt(jnp.mean(x * x, axis=-1, keepdims=True) + EPS)
    o_ref[...] = ((x * r * g_ref[...]) * (1.0 + sc_ref[0]) + sh_ref[0]).astype(o_ref.dtype)


def _mod_norm(x, gain, scale, shift):
    tm = 256
    per_batch = SEQ // tm
    row = lambda i: (i // per_batch, 0, 0)
    return pl.pallas_call(
        _norm_kernel,
        grid=(ROWS // tm,),
        in_specs=[
            pl.BlockSpec((tm, D_MODEL), lambda i: (i, 0)),
            pl.BlockSpec((1, D_MODEL), lambda i: (0, 0)),
            pl.BlockSpec((1, 1, D_MODEL), row),
            pl.BlockSpec((1, 1, D_MODEL), row),
        ],
        out_specs=pl.BlockSpec((tm, D_MODEL), lambda i: (i, 0)),
        out_shape=jax.ShapeDtypeStruct((ROWS, D_MODEL), BF16),
        compiler_params=_params("arbitrary"),
        name="mod_norm",
    )(x, gain.reshape(1, D_MODEL), scale, shift)


def _final_norm_kernel(x_ref, g_ref, o_ref):
    x = x_ref[...]
    r = lax.rsqrt(jnp.mean(x * x, axis=-1, keepdims=True) + EPS)
    o_ref[...] = x * r * g_ref[...]


def _final_norm(x, gain):
    tm = 256
    return pl.pallas_call(
        _final_norm_kernel,
        grid=(ROWS // tm,),
        in_specs=[
            pl.BlockSpec((tm, D_MODEL), lambda i: (i, 0)),
            pl.BlockSpec((1, D_MODEL), lambda i: (0, 0)),
        ],
        out_specs=pl.BlockSpec((tm, D_MODEL), lambda i: (i, 0)),
        out_shape=jax.ShapeDtypeStruct((ROWS, D_MODEL), F32),
        compiler_params=_params("arbitrary"),
        name="final_norm",
    )(x, gain.reshape(1, D_MODEL))


def _proj_kernel(a_ref, w_ref, o_ref, *, scale):
    acc = _dot(a_ref[...], w_ref[...])
    if scale != 1.0:
        acc = acc * scale
    o_ref[...] = acc.astype(o_ref.dtype)


def _proj(a, w, out_dtype, scale=1.0, tm=1024, tn=512):
    m, k = a.shape
    n = w.shape[1]
    return pl.pallas_call(
        functools.partial(_proj_kernel, scale=scale),
        grid=(m // tm, n // tn),
        in_specs=[
            pl.BlockSpec((tm, k), lambda i, j: (i, 0)),
            pl.BlockSpec((k, tn), lambda i, j: (0, j)),
        ],
        out_specs=pl.BlockSpec((tm, tn), lambda i, j: (i, j)),
        out_shape=jax.ShapeDtypeStruct((m, n), out_dtype),
        compiler_params=_params("arbitrary", "arbitrary"),
        name="proj",
    )(a, w)


def _pool_kernel(u_ref, w_ref, s_ref, o_ref, acc_ref):
    g = pl.program_id(1)
    x = u_ref[...]
    row = lax.broadcasted_iota(jnp.int32, (SEQ, 1), 0)

    def lagged(v, lag):
        return jnp.where(row >= lag, pltpu.roll(v, lag, axis=0), 0.0)

    acc_ref[...] = x + lagged(x, 1)
    for level in range(1, N_GROUPS):
        @pl.when(g >= level)
        def _():
            v = acc_ref[...]
            acc_ref[...] = v + lagged(v, 1 << level)

    window = jnp.left_shift(2, g)
    count = jnp.minimum(row + 1, window).astype(F32)
    pooled = acc_ref[...] / count - x
    y = _dot(pooled.astype(BF16), w_ref[0])
    o_ref[...] = (y * s_ref[...]).astype(o_ref.dtype)


def _pool_branch(u, w_mix, scale):
    return pl.pallas_call(
        _pool_kernel,
        grid=(BATCH, N_GROUPS),
        in_specs=[
            pl.BlockSpec((SEQ, POOL_GROUP), lambda b, g: (b, g)),
            pl.BlockSpec((1, POOL_GROUP, POOL_GROUP), lambda b, g: (g, 0, 0)),
            pl.BlockSpec((1, POOL_GROUP), lambda b, g: (0, g)),
        ],
        out_specs=pl.BlockSpec((SEQ, POOL_GROUP), lambda b, g: (b, g)),
        out_shape=jax.ShapeDtypeStruct((ROWS, POOL_WIDTH), BF16),
        scratch_shapes=[pltpu.VMEM((SEQ, POOL_GROUP), F32)],
        compiler_params=_params("arbitrary", "arbitrary"),
        name="pool_branch",
    )(u, w_mix, scale.reshape(1, POOL_WIDTH))


def _bias_kernel(rel_ref, bucket_ref, o_ref):
    head = (pl.program_id(0) * Q_PER_KV + pl.program_id(1) * 2 + pl.program_id(2))
    bucket = bucket_ref[0]
    acc = jnp.full(bucket.shape, NEG, F32)
    for b in range(N_BUCKETS):
        acc = jnp.where(bucket == b, rel_ref[b, head], acc)
    o_ref[0, 0] = acc


def _band_bias(rel_bias, bucket_pair):
    pairs = Q_PER_KV // 2
    return pl.pallas_call(
        _bias_kernel,
        grid=(N_KV_HEADS, pairs, 2, 2),
        in_specs=[
            pl.BlockSpec(memory_space=pltpu.SMEM),
            pl.BlockSpec((1, BLOCK, 2 * BLOCK), lambda h, p, e, f: (f, 0, 0)),
        ],
        out_specs=pl.BlockSpec((1, 1, BLOCK, 2 * BLOCK), lambda h, p, e, f: (f, h, p, e)),
        out_shape=jax.ShapeDtypeStruct((2, N_KV_HEADS, pairs * BLOCK, 4 * BLOCK), F32),
        compiler_params=_params("arbitrary", "arbitrary", "arbitrary", "arbitrary"),
        name="band_bias",
    )(rel_bias, bucket_pair)


def _bucket_tables():
    i = jnp.arange(BLOCK)[:, None]
    j = jnp.arange(2 * BLOCK)[None, :]
    dist = i + BLOCK - j
    max_exact = N_BUCKETS // 2
    d = jnp.maximum(dist, 0)
    log_ratio = jnp.log(jnp.maximum(d, 1).astype(F32) / max_exact) / math.log(MAX_DISTANCE / max_exact)
    large = jnp.minimum(max_exact + (log_ratio * (N_BUCKETS - max_exact)).astype(jnp.int32), N_BUCKETS - 1)
    bucket = jnp.where(d < max_exact, d, large)
    in_window = (dist >= 0) & (dist < WINDOW)
    later = jnp.where(in_window, bucket, -1)
    first = jnp.where(j >= BLOCK, later, -1)
    return jnp.stack([first, later]).astype(jnp.int32)


def _attn_kernel(sink_ref, q_ref, kvp_ref, kvc_ref, bias_ref, o_ref):
    pairs = Q_PER_KV // 2
    lane = lax.broadcasted_iota(jnp.int32, (1, 2 * HEAD_DIM), 1)
    low = lane < HEAD_DIM
    for h in range(N_KV_HEADS):
        slab = (h // 2) * 2 * HEAD_DIM
        k_band = jnp.concatenate([kvp_ref[:, slab:slab + 2 * HEAD_DIM],
                                  kvc_ref[:, slab:slab + 2 * HEAD_DIM]], axis=0)
        v_band = jnp.concatenate([kvp_ref[:, KV_WIDTH + slab:KV_WIDTH + slab + 2 * HEAD_DIM],
                                  kvc_ref[:, KV_WIDTH + slab:KV_WIDTH + slab + 2 * HEAD_DIM]], axis=0)
        k_swap = pltpu.roll(k_band, HEAD_DIM, axis=1)
        v_swap = pltpu.roll(v_band, HEAD_DIM, axis=1)
        k_low, k_high = (k_band, k_swap) if h % 2 == 0 else (k_swap, k_band)
        v_low, v_high = (v_band, v_swap) if h % 2 == 0 else (v_swap, v_band)
        zero = jnp.zeros_like(v_band)
        v_diag = jnp.concatenate([jnp.where(low, v_low, zero), jnp.where(low, zero, v_high)], axis=0)

        q4 = jnp.concatenate(
            [q_ref[:, h * Q_PER_KV * HEAD_DIM + p * 2 * HEAD_DIM:
                   h * Q_PER_KV * HEAD_DIM + (p + 1) * 2 * HEAD_DIM] for p in range(pairs)], axis=0)
        qz = jnp.zeros_like(q4)
        dims = (((1,), (1,)), ((), ()))
        logit_even = lax.dot_general(jnp.where(low, q4, qz), k_low, dims, preferred_element_type=F32)
        logit_odd = lax.dot_general(jnp.where(low, qz, q4), k_high, dims, preferred_element_type=F32)
        logits = jnp.concatenate([logit_even, logit_odd], axis=1) + bias_ref[0, h]

        prob_rows = []
        for p in range(pairs):
            halves = []
            for e in range(2):
                sink = sink_ref[h * Q_PER_KV + 2 * p + e]
                blk = logits[p * BLOCK:(p + 1) * BLOCK, e * 2 * BLOCK:(e + 1) * 2 * BLOCK]
                m = jnp.maximum(jnp.max(blk, axis=1, keepdims=True), sink)
                ex = jnp.exp(blk - m)
                denom = jnp.sum(ex, axis=1, keepdims=True) + jnp.exp(sink - m)
                halves.append((ex * (1.0 / denom)).astype(BF16))
            prob_rows.append(jnp.concatenate(halves, axis=1))
        probs = jnp.concatenate(prob_rows, axis=0)
        out = _dot(probs, v_diag)
        for p in range(pairs):
            c0 = h * Q_PER_KV * HEAD_DIM + p * 2 * HEAD_DIM
            o_ref[:, c0:c0 + 2 * HEAD_DIM] = out[p * BLOCK:(p + 1) * BLOCK].astype(o_ref.dtype)


def _attention(q, kv, sinks, bias):
    n_steps = ROWS // BLOCK
    return pl.pallas_call(
        _attn_kernel,
        grid=(n_steps,),
        in_specs=[
            pl.BlockSpec(memory_space=pltpu.SMEM),
            pl.BlockSpec((BLOCK, Q_WIDTH), lambda r: (r, 0)),
            pl.BlockSpec((BLOCK, 2 * KV_WIDTH), lambda r: (jnp.maximum(r - 1, 0), 0)),
            pl.BlockSpec((BLOCK, 2 * KV_WIDTH), lambda r: (r, 0)),
            pl.BlockSpec((1, N_KV_HEADS, (Q_PER_KV // 2) * BLOCK, 4 * BLOCK),
                         lambda r: (jnp.minimum(r % N_BLOCKS, 1), 0, 0, 0)),
        ],
        out_specs=pl.BlockSpec((BLOCK, Q_WIDTH), lambda r: (r, 0)),
        out_shape=jax.ShapeDtypeStruct((ROWS, Q_WIDTH), BF16),
        compiler_params=_params("arbitrary"),
        name="swa_attention",
    )(sinks, q, kv, kv, bias)


def _merge_kernel(h_ref, yp_ref, ya_ref, wgp_ref, wga_ref, wbp_ref, wba_ref, o_ref):
    h = h_ref[...]
    gate_pool = jax.nn.sigmoid(_dot(h, wgp_ref[...]))
    gate_attn = jax.nn.sigmoid(_dot(h, wga_ref[...]))
    y_pool = _dot(yp_ref[...], wbp_ref[...])
    y_attn = _dot(ya_ref[...], wba_ref[...])
    o_ref[...] = (gate_pool * y_pool + gate_attn * y_attn).astype(o_ref.dtype)


def _merge(h, yp, ya, w_gate, w_bp, w_ba, tm=512, tn=512):
    nj = D_MODEL // tn
    return pl.pallas_call(
        _merge_kernel,
        grid=(ROWS // tm, nj),
        in_specs=[
            pl.BlockSpec((tm, D_MODEL), lambda i, j: (i, 0)),
            pl.BlockSpec((tm, POOL_WIDTH), lambda i, j: (i, 0)),
            pl.BlockSpec((tm, Q_WIDTH), lambda i, j: (i, 0)),
            pl.BlockSpec((D_MODEL, tn), lambda i, j: (0, j)),
            pl.BlockSpec((D_MODEL, tn), lambda i, j: (0, j + nj)),
            pl.BlockSpec((POOL_WIDTH, tn), lambda i, j: (0, j)),
            pl.BlockSpec((Q_WIDTH, tn), lambda i, j: (0, j)),
        ],
        out_specs=pl.BlockSpec((tm, tn), lambda i, j: (i, j)),
        out_shape=jax.ShapeDtypeStruct((ROWS, D_MODEL), BF16),
        compiler_params=_params("arbitrary", "arbitrary"),
        name="gated_merge",
    )(h, yp, ya, w_gate, w_gate, w_bp, w_ba)


def _resid_kernel(a_ref, w_ref, x_ref, g_ref, o_ref, acc_ref):
    k = pl.program_id(2)

    @pl.when(k == 0)
    def _():
        acc_ref[...] = jnp.zeros_like(acc_ref)

    acc_ref[...] += _dot(a_ref[...], w_ref[...])

    @pl.when(k == pl.num_programs(2) - 1)
    def _():
        o_ref[...] = x_ref[...] + g_ref[0] * acc_ref[...]


def _resid_proj(a, w, x, gate, tm, tn, tk):
    m, kdim = a.shape
    n = w.shape[1]
    per_batch = SEQ // tm
    return pl.pallas_call(
        _resid_kernel,
        grid=(m // tm, n // tn, kdim // tk),
        in_specs=[
            pl.BlockSpec((tm, tk), lambda i, j, k: (i, k)),
            pl.BlockSpec((tk, tn), lambda i, j, k: (k, j)),
            pl.BlockSpec((tm, tn), lambda i, j, k: (i, j)),
            pl.BlockSpec((1, 1, tn), lambda i, j, k: (i // per_batch, 0, j)),
        ],
        out_specs=pl.BlockSpec((tm, tn), lambda i, j, k: (i, j)),
        out_shape=jax.ShapeDtypeStruct((m, n), F32),
        scratch_shapes=[pltpu.VMEM((tm, tn), F32)],
        compiler_params=_params("arbitrary", "arbitrary", "arbitrary"),
        name="resid_proj",
    )(a, w, x, gate)


def _swiglu_kernel(a_ref, w_ref, o_ref):
    ab = _dot(a_ref[...], w_ref[...])
    a = ab[:, :FF_TILE]
    b = ab[:, FF_TILE:]
    o_ref[...] = ((a * jax.nn.sigmoid(a)) * b).astype(o_ref.dtype)


def _swiglu(h2, w_ab, tm=1024):
    n_tiles = D_FF_PAD // FF_TILE
    return pl.pallas_call(
        _swiglu_kernel,
        grid=(ROWS // tm, n_tiles),
        in_specs=[
            pl.BlockSpec((tm, D_MODEL), lambda i, j: (i, 0)),
            pl.BlockSpec((D_MODEL, 2 * FF_TILE), lambda i, j: (0, j)),
        ],
        out_specs=pl.BlockSpec((tm, FF_TILE), lambda i, j: (i, j)),
        out_shape=jax.ShapeDtypeStruct((ROWS, D_FF_PAD), BF16),
        compiler_params=_params("arbitrary", "arbitrary"),
        name="swiglu_up",
    )(h2, w_ab)


def _pack_ffn_in(w):
    pad = ((0, 0), (0, D_FF_PAD - D_FF))
    a = jnp.pad(w[:, :D_FF].astype(BF16), pad).reshape(D_MODEL, -1, FF_TILE)
    b = jnp.pad(w[:, D_FF:].astype(BF16), pad).reshape(D_MODEL, -1, FF_TILE)
    return jnp.concatenate([a, b], axis=2).reshape(D_MODEL, 2 * D_FF_PAD)


def kernel(x, c, w_ada, b_ada, norm1, w_in, w_pool_mix, pool_scale, sinks, rel_bias,
           w_branch_pool, w_branch_attn, w_gate, w_out, norm2, w_ffn_in, w_ffn_out,
           final_norm):
    c_pad = jnp.pad(c, ((0, 8 - BATCH), (0, 0)))
    mod = _ada(c_pad, w_ada, b_ada)[:, :BATCH]
    bias = _band_bias(rel_bias, _bucket_tables())
    xr = x.reshape(ROWS, D_MODEL)

    for l in range(DEPTH):
        sh1, sc1, g1, sh2, sc2, g2 = [m.reshape(BATCH, 1, D_MODEL)
                                      for m in jnp.split(mod[l], N_MOD, axis=-1)]
        w_in_l = w_in[l].astype(BF16)
        h = _mod_norm(xr, norm1[l], sc1, sh1)
        u = _proj(h, w_in_l[:, :POOL_WIDTH], F32)
        q = _proj(h, w_in_l[:, POOL_WIDTH:POOL_WIDTH + Q_WIDTH], BF16, scale=HEAD_DIM ** -0.5)
        kv = _proj(h, w_in_l[:, POOL_WIDTH + Q_WIDTH:], BF16)
        yp = _pool_branch(u, w_pool_mix[l].astype(BF16), pool_scale[l])
        ya = _attention(q, kv, sinks[l], bias)
        merged = _merge(h, yp, ya, w_gate[l].astype(BF16),
                        w_branch_pool[l].astype(BF16), w_branch_attn[l].astype(BF16))
        xr = _resid_proj(merged, w_out[l].astype(BF16), xr, g1, tm=1024, tn=1024, tk=2048)

        h2 = _mod_norm(xr, norm2[l], sc2, sh2)
        act = _swiglu(h2, _pack_ffn_in(w_ffn_in[l]))
        w_down = jnp.pad(w_ffn_out[l].astype(BF16), ((0, D_FF_PAD - D_FF), (0, 0)))
        xr = _resid_proj(act, w_down, xr, g2, tm=1024, tn=1024, tk=D_FF_PAD // 4)

    return _final_norm(xr, final_norm).reshape(BATCH, SEQ, D_MODEL)
```

```python
import functools
import math

import jax
import jax.numpy as jnp
from jax import lax
from jax.experimental import pallas as pl
from jax.experimental.pallas import tpu as pltpu

D_MODEL = 4096
BATCH = 4
SEQ = 2048
ROWS = BATCH * SEQ
DEPTH = 2
POOL_WINDOWS = (2, 4, 8, 16)
N_GROUPS = len(POOL_WINDOWS)
POOL_WIDTH = D_MODEL // 2
POOL_GROUP = POOL_WIDTH // N_GROUPS
HEAD_DIM = 64
N_Q_HEADS = (D_MODEL // 2) // HEAD_DIM
N_KV_HEADS = N_Q_HEADS // 8
Q_PER_KV = N_Q_HEADS // N_KV_HEADS
Q_WIDTH = N_Q_HEADS * HEAD_DIM
KV_WIDTH = N_KV_HEADS * HEAD_DIM
WINDOW = 128
BLOCK = 128
N_BLOCKS = SEQ // BLOCK
N_BUCKETS = 32
MAX_DISTANCE = 128
IN_WIDTH = POOL_WIDTH + Q_WIDTH + 2 * KV_WIDTH
D_FF = -(-(8 * D_MODEL) // (3 * 256)) * 256
N_MOD = 6
EPS = 1e-6

V7X_VMEM_BYTES = 64 * 1024 * 1024
VMEM_LIMIT = V7X_VMEM_BYTES - 8 * 1024 * 1024
V7X_MXU_DIM = 256

NEG = -1e30

BF16 = jnp.bfloat16
F32 = jnp.float32


def _params(*sem):
    return pltpu.CompilerParams(dimension_semantics=sem, vmem_limit_bytes=VMEM_LIMIT)


def _dot(a, b):
    return jnp.dot(a, b, preferred_element_type=F32)


ADA_ROWS = 128


def _ada_kernel(c_ref, w_ref, b_ref, o_ref):
    k = pl.program_id(1)

    @pl.when(k == 0)
    def _():
        o_ref[0] = jnp.broadcast_to(b_ref[0], o_ref.shape[1:])

    c = c_ref[...]
    s = (c * jax.nn.sigmoid(c)).astype(BF16)
    o_ref[0] += _dot(s, w_ref[0].astype(BF16))


def _ada(c_pad, w_ada, b_ada):
    n = N_MOD * D_MODEL
    return pl.pallas_call(
        _ada_kernel,
        grid=(DEPTH, D_MODEL // ADA_ROWS),
        in_specs=[
            pl.BlockSpec((8, ADA_ROWS), lambda l, k: (0, k)),
            pl.BlockSpec((1, ADA_ROWS, n), lambda l, k: (l, k, 0)),
            pl.BlockSpec((1, 1, n), lambda l, k: (l, 0, 0)),
        ],
        out_specs=pl.BlockSpec((1, 8, n), lambda l, k: (l, 0, 0)),
        out_shape=jax.ShapeDtypeStruct((DEPTH, 8, n), F32),
        compiler_params=_params("arbitrary", "arbitrary"),
        name="ada_mod",
    )(c_pad, w_ada, b_ada.reshape(DEPTH, 1, n))


def _norm_kernel(x_ref, g_ref, sc_ref, sh_ref, o_ref):
    x = x_ref[...]
    r = lax.rsqrt(jnp.mean(x * x, axis=-1, keepdims=True) + EPS)
    o_ref[...] = ((x * r * g_ref[0]) * (1.0 + sc_ref[0]) + sh_ref[0]).astype(o_ref.dtype)


def _mod_norm(x, gain, layer, scale, shift):
    tm = 256
    per_batch = SEQ // tm
    row = lambda i: (i // per_batch, 0, 0)
    return pl.pallas_call(
        _norm_kernel,
        grid=(ROWS // tm,),
        in_specs=[
            pl.BlockSpec((tm, D_MODEL), lambda i: (i, 0)),
            pl.BlockSpec((1, 1, D_MODEL), lambda i: (layer, 0, 0)),
            pl.BlockSpec((1, 1, D_MODEL), row),
            pl.BlockSpec((1, 1, D_MODEL), row),
        ],
        out_specs=pl.BlockSpec((tm, D_MODEL), lambda i: (i, 0)),
        out_shape=jax.ShapeDtypeStruct((ROWS, D_MODEL), BF16),
        compiler_params=_params("arbitrary"),
        name="mod_norm",
    )(x, gain.reshape(DEPTH, 1, D_MODEL), scale, shift)


def _final_norm_kernel(x_ref, g_ref, o_ref):
    x = x_ref[...]
    r = lax.rsqrt(jnp.mean(x * x, axis=-1, keepdims=True) + EPS)
    o_ref[...] = x * r * g_ref[...]


def _final_norm(x, gain):
    tm = 256
    return pl.pallas_call(
        _final_norm_kernel,
        grid=(ROWS // tm,),
        in_specs=[
            pl.BlockSpec((tm, D_MODEL), lambda i: (i, 0)),
            pl.BlockSpec((1, D_MODEL), lambda i: (0, 0)),
        ],
        out_specs=pl.BlockSpec((tm, D_MODEL), lambda i: (i, 0)),
        out_shape=jax.ShapeDtypeStruct((ROWS, D_MODEL), F32),
        compiler_params=_params("arbitrary"),
        name="final_norm",
    )(x, gain.reshape(1, D_MODEL))


def _cast_chunk(w_ref, scr, slot, step, col0=0):
    rows, cols = w_ref.shape[1], w_ref.shape[2]
    r0 = pl.multiple_of(step * rows, rows)
    scr[slot, pl.ds(r0, rows), col0:col0 + cols] = w_ref[0].astype(BF16)


def _ws_ids():
    j = pl.program_id(0)
    return j, pl.program_id(1), j % 2


def _w_spec(layer, rows, cols, n_tiles, col_tile0=0, row_chunk0=0):
    return pl.BlockSpec((1, rows, cols),
                        lambda j, i: (layer, row_chunk0 + i, col_tile0 + jnp.minimum(j, n_tiles - 1)))


def _a_spec(tm, k, col_block=0):
    return pl.BlockSpec((tm, k), lambda j, i: (jnp.where(j > 0, i, 0), col_block))


def _o_spec(tm, tn):
    return pl.BlockSpec((tm, tn), lambda j, i: (jnp.where(j > 0, i, 0), jnp.maximum(j - 1, 0)))


def _proj_kernel(a_ref, w_ref, o_ref, scr, *, scale):
    j, i, slot = _ws_ids()

    @pl.when(j == 0)
    def _():
        _cast_chunk(w_ref, scr, slot, i)

    @pl.when(j > 0)
    def _():
        _cast_chunk(w_ref, scr, slot, i)
        acc = _dot(a_ref[...], scr[1 - slot])
        if scale != 1.0:
            acc = acc * scale
        o_ref[...] = acc.astype(o_ref.dtype)


def _proj(a, w, layer, col0, n, out_dtype, scale=1.0, tm=1024, tn=512):
    m, k = a.shape
    ni, nj = m // tm, n // tn
    return pl.pallas_call(
        functools.partial(_proj_kernel, scale=scale),
        grid=(nj + 1, ni),
        in_specs=[_a_spec(tm, k), _w_spec(layer, k // ni, tn, nj, col_tile0=col0 // tn)],
        out_specs=_o_spec(tm, tn),
        out_shape=jax.ShapeDtypeStruct((m, n), out_dtype),
        scratch_shapes=[pltpu.VMEM((2, k, tn), BF16)],
        compiler_params=_params("arbitrary", "arbitrary"),
        name="proj",
    )(a, w)


def _pool_kernel(u_ref, w_ref, s_ref, o_ref, acc_ref):
    g = pl.program_id(1)
    x = u_ref[...]
    row = lax.broadcasted_iota(jnp.int32, (SEQ, 1), 0)

    def lagged(v, lag):
        return jnp.where(row >= lag, pltpu.roll(v, lag, axis=0), 0.0)

    acc_ref[...] = x + lagged(x, 1)
    for level in range(1, N_GROUPS):
        @pl.when(g >= level)
        def _():
            v = acc_ref[...]
            acc_ref[...] = v + lagged(v, 1 << level)

    window = jnp.left_shift(2, g)
    count = jnp.minimum(row + 1, window).astype(F32)
    pooled = acc_ref[...] / count - x
    y = _dot(pooled.astype(BF16), w_ref[0, 0].astype(BF16))
    o_ref[...] = (y * s_ref[0]).astype(o_ref.dtype)


def _pool_branch(u, w_mix, scale, layer):
    return pl.pallas_call(
        _pool_kernel,
        grid=(BATCH, N_GROUPS),
        in_specs=[
            pl.BlockSpec((SEQ, POOL_GROUP), lambda b, g: (b, g)),
            pl.BlockSpec((1, 1, POOL_GROUP, POOL_GROUP), lambda b, g: (layer, g, 0, 0)),
            pl.BlockSpec((1, 1, POOL_GROUP), lambda b, g: (layer, 0, g)),
        ],
        out_specs=pl.BlockSpec((SEQ, POOL_GROUP), lambda b, g: (b, g)),
        out_shape=jax.ShapeDtypeStruct((ROWS, POOL_WIDTH), BF16),
        scratch_shapes=[pltpu.VMEM((SEQ, POOL_GROUP), F32)],
        compiler_params=_params("arbitrary", "arbitrary"),
        name="pool_branch",
    )(u, w_mix, scale.reshape(DEPTH, 1, POOL_WIDTH))


def _bias_kernel(rel_ref, bucket_ref, o_ref):
    head = (pl.program_id(0) * Q_PER_KV + pl.program_id(1) * 2 + pl.program_id(2))
    bucket = bucket_ref[0]
    acc = jnp.full(bucket.shape, NEG, F32)
    for b in range(N_BUCKETS):
        acc = jnp.where(bucket == b, rel_ref[b, head], acc)
    o_ref[0, 0] = acc


def _band_bias(rel_bias, bucket_pair):
    pairs = Q_PER_KV // 2
    return pl.pallas_call(
        _bias_kernel,
        grid=(N_KV_HEADS, pairs, 2, 2),
        in_specs=[
            pl.BlockSpec(memory_space=pltpu.SMEM),
            pl.BlockSpec((1, BLOCK, 2 * BLOCK), lambda h, p, e, f: (f, 0, 0)),
        ],
        out_specs=pl.BlockSpec((1, 1, BLOCK, 2 * BLOCK), lambda h, p, e, f: (f, h, p, e)),
        out_shape=jax.ShapeDtypeStruct((2, N_KV_HEADS, pairs * BLOCK, 4 * BLOCK), F32),
        compiler_params=_params("arbitrary", "arbitrary", "arbitrary", "arbitrary"),
        name="band_bias",
    )(rel_bias, bucket_pair)


def _bucket_tables():
    i = jnp.arange(BLOCK)[:, None]
    j = jnp.arange(2 * BLOCK)[None, :]
    dist = i + BLOCK - j
    max_exact = N_BUCKETS // 2
    d = jnp.maximum(dist, 0)
    log_ratio = jnp.log(jnp.maximum(d, 1).astype(F32) / max_exact) / math.log(MAX_DISTANCE / max_exact)
    large = jnp.minimum(max_exact + (log_ratio * (N_BUCKETS - max_exact)).astype(jnp.int32), N_BUCKETS - 1)
    bucket = jnp.where(d < max_exact, d, large)
    in_window = (dist >= 0) & (dist < WINDOW)
    later = jnp.where(in_window, bucket, -1)
    first = jnp.where(j >= BLOCK, later, -1)
    return jnp.stack([first, later]).astype(jnp.int32)


def _attn_kernel(sink_ref, q_ref, kvp_ref, kvc_ref, bias_ref, o_ref, *, layer):
    pairs = Q_PER_KV // 2
    lane = lax.broadcasted_iota(jnp.int32, (1, 2 * HEAD_DIM), 1)
    low = lane < HEAD_DIM
    for h in range(N_KV_HEADS):
        slab = (h // 2) * 2 * HEAD_DIM
        k_band = jnp.concatenate([kvp_ref[:, slab:slab + 2 * HEAD_DIM],
                                  kvc_ref[:, slab:slab + 2 * HEAD_DIM]], axis=0)
        v_band = jnp.concatenate([kvp_ref[:, KV_WIDTH + slab:KV_WIDTH + slab + 2 * HEAD_DIM],
                                  kvc_ref[:, KV_WIDTH + slab:KV_WIDTH + slab + 2 * HEAD_DIM]], axis=0)
        k_swap = pltpu.roll(k_band, HEAD_DIM, axis=1)
        v_swap = pltpu.roll(v_band, HEAD_DIM, axis=1)
        k_low, k_high = (k_band, k_swap) if h % 2 == 0 else (k_swap, k_band)
        v_low, v_high = (v_band, v_swap) if h % 2 == 0 else (v_swap, v_band)
        zero = jnp.zeros_like(v_band)
        v_diag = jnp.concatenate([jnp.where(low, v_low, zero), jnp.where(low, zero, v_high)], axis=0)

        q4 = jnp.concatenate(
            [q_ref[:, h * Q_PER_KV * HEAD_DIM + p * 2 * HEAD_DIM:
                   h * Q_PER_KV * HEAD_DIM + (p + 1) * 2 * HEAD_DIM] for p in range(pairs)], axis=0)
        qz = jnp.zeros_like(q4)
        dims = (((1,), (1,)), ((), ()))
        logit_even = lax.dot_general(jnp.where(low, q4, qz), k_low, dims, preferred_element_type=F32)
        logit_odd = lax.dot_general(jnp.where(low, qz, q4), k_high, dims, preferred_element_type=F32)
        logits = jnp.concatenate([logit_even, logit_odd], axis=1) + bias_ref[0, h]

        prob_rows = []
        for p in range(pairs):
            halves = []
            for e in range(2):
                sink = sink_ref[layer, h * Q_PER_KV + 2 * p + e]
                blk = logits[p * BLOCK:(p + 1) * BLOCK, e * 2 * BLOCK:(e + 1) * 2 * BLOCK]
                m = jnp.maximum(jnp.max(blk, axis=1, keepdims=True), sink)
                ex = jnp.exp(blk - m)
                denom = jnp.sum(ex, axis=1, keepdims=True) + jnp.exp(sink - m)
                halves.append((ex * (1.0 / denom)).astype(BF16))
            prob_rows.append(jnp.concatenate(halves, axis=1))
        probs = jnp.concatenate(prob_rows, axis=0)
        out = _dot(probs, v_diag)
        for p in range(pairs):
            c0 = h * Q_PER_KV * HEAD_DIM + p * 2 * HEAD_DIM
            o_ref[:, c0:c0 + 2 * HEAD_DIM] = out[p * BLOCK:(p + 1) * BLOCK].astype(o_ref.dtype)


def _attention(q, kv, sinks, bias, layer):
    n_steps = ROWS // BLOCK
    return pl.pallas_call(
        functools.partial(_attn_kernel, layer=layer),
        grid=(n_steps,),
        in_specs=[
            pl.BlockSpec(memory_space=pltpu.SMEM),
            pl.BlockSpec((BLOCK, Q_WIDTH), lambda r: (r, 0)),
            pl.BlockSpec((BLOCK, 2 * KV_WIDTH), lambda r: (jnp.maximum(r - 1, 0), 0)),
            pl.BlockSpec((BLOCK, 2 * KV_WIDTH), lambda r: (r, 0)),
            pl.BlockSpec((1, N_KV_HEADS, (Q_PER_KV // 2) * BLOCK, 4 * BLOCK),
                         lambda r: (jnp.minimum(r % N_BLOCKS, 1), 0, 0, 0)),
        ],
        out_specs=pl.BlockSpec((BLOCK, Q_WIDTH), lambda r: (r, 0)),
        out_shape=jax.ShapeDtypeStruct((ROWS, Q_WIDTH), BF16),
        compiler_params=_params("arbitrary"),
        name="swa_attention",
    )(sinks, q, kv, kv, bias)


def _merge_kernel(h_ref, yp_ref, ya_ref, wgp_ref, wga_ref, wbp_ref, wba_ref, o_ref,
                  sgp, sga, sbp, sba):
    j, i, slot = _ws_ids()

    def cast():
        _cast_chunk(wgp_ref, sgp, slot, i)
        _cast_chunk(wga_ref, sga, slot, i)
        _cast_chunk(wbp_ref, sbp, slot, i)
        _cast_chunk(wba_ref, sba, slot, i)

    @pl.when(j == 0)
    def _():
        cast()

    @pl.when(j > 0)
    def _():
        cast()
        h = h_ref[...]
        gate_pool = jax.nn.sigmoid(_dot(h, sgp[1 - slot]))
        gate_attn = jax.nn.sigmoid(_dot(h, sga[1 - slot]))
        y_pool = _dot(yp_ref[...], sbp[1 - slot])
        y_attn = _dot(ya_ref[...], sba[1 - slot])
        o_ref[...] = (gate_pool * y_pool + gate_attn * y_attn).astype(o_ref.dtype)


def _merge(h, yp, ya, w_gate, w_bp, w_ba, layer, tm=512, tn=512):
    ni, nj = ROWS // tm, D_MODEL // tn
    return pl.pallas_call(
        _merge_kernel,
        grid=(nj + 1, ni),
        in_specs=[
            _a_spec(tm, D_MODEL), _a_spec(tm, POOL_WIDTH), _a_spec(tm, Q_WIDTH),
            _w_spec(layer, D_MODEL // ni, tn, nj),
            _w_spec(layer, D_MODEL // ni, tn, nj, col_tile0=nj),
            _w_spec(layer, POOL_WIDTH // ni, tn, nj),
            _w_spec(layer, Q_WIDTH // ni, tn, nj),
        ],
        out_specs=_o_spec(tm, tn),
        out_shape=jax.ShapeDtypeStruct((ROWS, D_MODEL), BF16),
        scratch_shapes=[pltpu.VMEM((2, D_MODEL, tn), BF16), pltpu.VMEM((2, D_MODEL, tn), BF16),
                        pltpu.VMEM((2, POOL_WIDTH, tn), BF16), pltpu.VMEM((2, Q_WIDTH, tn), BF16)],
        compiler_params=_params("arbitrary", "arbitrary"),
        name="gated_merge",
    )(h, yp, ya, w_gate, w_gate, w_bp, w_ba)


def _resid_kernel(a_ref, w_ref, x_ref, g_ref, o_ref, scr):
    j, i, slot = _ws_ids()

    @pl.when(j == 0)
    def _():
        _cast_chunk(w_ref, scr, slot, i)

    @pl.when(j > 0)
    def _():
        _cast_chunk(w_ref, scr, slot, i)
        o_ref[...] = x_ref[...] + g_ref[0] * _dot(a_ref[...], scr[1 - slot])


def _resid_proj(a, w, x, gate, layer, tm, tn):
    m, k = a.shape
    n = x.shape[1]
    ni, nj = m // tm, n // tn
    per_batch = SEQ // tm
    return pl.pallas_call(
        _resid_kernel,
        grid=(nj + 1, ni),
        in_specs=[
            _a_spec(tm, k),
            _w_spec(layer, k // ni, tn, nj),
            _o_spec(tm, tn),
            pl.BlockSpec((1, 1, tn), lambda j, i: (i // per_batch, 0, jnp.maximum(j - 1, 0))),
        ],
        out_specs=_o_spec(tm, tn),
        out_shape=jax.ShapeDtypeStruct((m, n), F32),
        scratch_shapes=[pltpu.VMEM((2, k, tn), BF16)],
        compiler_params=_params("arbitrary", "arbitrary"),
        name="resid_proj",
    )(a, w, x, gate)


FF_TILE = V7X_MXU_DIM


def _swiglu_kernel(a_ref, wa_ref, wb_ref, o_ref, scr):
    j, i, slot = _ws_ids()

    def cast():
        _cast_chunk(wa_ref, scr, slot, i)
        _cast_chunk(wb_ref, scr, slot, i, col0=FF_TILE)

    @pl.when(j == 0)
    def _():
        cast()

    @pl.when(j > 0)
    def _():
        cast()
        ab = _dot(a_ref[...], scr[1 - slot])
        a = ab[:, :FF_TILE]
        b = ab[:, FF_TILE:]
        o_ref[...] = ((a * jax.nn.sigmoid(a)) * b).astype(o_ref.dtype)


def _swiglu(h2, w_ffn_in, layer, tm=1024):
    ni, nj = ROWS // tm, D_FF // FF_TILE
    return pl.pallas_call(
        _swiglu_kernel,
        grid=(nj + 1, ni),
        in_specs=[
            _a_spec(tm, D_MODEL),
            _w_spec(layer, D_MODEL // ni, FF_TILE, nj),
            _w_spec(layer, D_MODEL // ni, FF_TILE, nj, col_tile0=nj),
        ],
        out_specs=_o_spec(tm, FF_TILE),
        out_shape=jax.ShapeDtypeStruct((ROWS, D_FF), BF16),
        scratch_shapes=[pltpu.VMEM((2, D_MODEL, 2 * FF_TILE), BF16)],
        compiler_params=_params("arbitrary", "arbitrary"),
        name="swiglu_up",
    )(h2, w_ffn_in, w_ffn_in)


def kernel(x, c, w_ada, b_ada, norm1, w_in, w_pool_mix, pool_scale, sinks, rel_bias,
           w_branch_pool, w_branch_attn, w_gate, w_out, norm2, w_ffn_in, w_ffn_out,
           final_norm):
    c_pad = jnp.pad(c, ((0, 8 - BATCH), (0, 0)))
    mod = _ada(c_pad, w_ada, b_ada)[:, :BATCH]
    bias = _band_bias(rel_bias, _bucket_tables())
    xr = x.reshape(ROWS, D_MODEL)

    for l in range(DEPTH):
        sh1, sc1, g1, sh2, sc2, g2 = [m.reshape(BATCH, 1, D_MODEL)
                                      for m in jnp.split(mod[l], N_MOD, axis=-1)]
        h = _mod_norm(xr, norm1, l, sc1, sh1)
        u = _proj(h, w_in, l, 0, POOL_WIDTH, F32)
        q = _proj(h, w_in, l, POOL_WIDTH, Q_WIDTH, BF16, scale=HEAD_DIM ** -0.5)
        kv = _proj(h, w_in, l, POOL_WIDTH + Q_WIDTH, 2 * KV_WIDTH, BF16)
        yp = _pool_branch(u, w_pool_mix, pool_scale, l)
        ya = _attention(q, kv, sinks, bias, l)
        merged = _merge(h, yp, ya, w_gate, w_branch_pool, w_branch_attn, l)
        xr = _resid_proj(merged, w_out, xr, g1, l, tm=1024, tn=512)

        h2 = _mod_norm(xr, norm2, l, sc2, sh2)
        act = _swiglu(h2, w_ffn_in, l)
        xr = _resid_proj(act, w_ffn_out, xr, g2, l, tm=512, tn=512)

    return _final_norm(xr, final_norm).reshape(BATCH, SEQ, D_MODEL)
```

```python
import functools
import math

import jax
import jax.numpy as jnp
from jax import lax
from jax.experimental import pallas as pl
from jax.experimental.pallas import tpu as pltpu

D_MODEL = 4096
BATCH = 4
SEQ = 2048
ROWS = BATCH * SEQ
DEPTH = 2
POOL_WINDOWS = (2, 4, 8, 16)
N_GROUPS = len(POOL_WINDOWS)
POOL_WIDTH = D_MODEL // 2
POOL_GROUP = POOL_WIDTH // N_GROUPS
HEAD_DIM = 64
N_Q_HEADS = (D_MODEL // 2) // HEAD_DIM
N_KV_HEADS = N_Q_HEADS // 8
Q_PER_KV = N_Q_HEADS // N_KV_HEADS
Q_WIDTH = N_Q_HEADS * HEAD_DIM
KV_WIDTH = N_KV_HEADS * HEAD_DIM
WINDOW = 128
BLOCK = 128
N_BLOCKS = SEQ // BLOCK
N_BUCKETS = 32
MAX_DISTANCE = 128
IN_WIDTH = POOL_WIDTH + Q_WIDTH + 2 * KV_WIDTH
D_FF = -(-(8 * D_MODEL) // (3 * 256)) * 256
N_MOD = 6
EPS = 1e-6

V7X_VMEM_BYTES = 64 * 1024 * 1024
VMEM_LIMIT = V7X_VMEM_BYTES - 8 * 1024 * 1024
V7X_MXU_DIM = 256

NEG = -1e30

BF16 = jnp.bfloat16
F32 = jnp.float32


def _params(*sem):
    return pltpu.CompilerParams(dimension_semantics=sem, vmem_limit_bytes=VMEM_LIMIT)


def _dot(a, b):
    return jnp.dot(a, b, preferred_element_type=F32)


ADA_ROWS = 128


def _ada_kernel(c_ref, w_ref, b_ref, o_ref):
    k = pl.program_id(1)

    @pl.when(k == 0)
    def _():
        o_ref[0] = jnp.broadcast_to(b_ref[0], o_ref.shape[1:])

    c = c_ref[...]
    s = (c * jax.nn.sigmoid(c)).astype(BF16)
    o_ref[0] += _dot(s, w_ref[0].astype(BF16))


def _ada(c_pad, w_ada, b_ada):
    n = N_MOD * D_MODEL
    return pl.pallas_call(
        _ada_kernel,
        grid=(DEPTH, D_MODEL // ADA_ROWS),
        in_specs=[
            pl.BlockSpec((8, ADA_ROWS), lambda l, k: (0, k)),
            pl.BlockSpec((1, ADA_ROWS, n), lambda l, k: (l, k, 0)),
            pl.BlockSpec((1, 1, n), lambda l, k: (l, 0, 0)),
        ],
        out_specs=pl.BlockSpec((1, 8, n), lambda l, k: (l, 0, 0)),
        out_shape=jax.ShapeDtypeStruct((DEPTH, 8, n), F32),
        compiler_params=_params("arbitrary", "arbitrary"),
        name="ada_mod",
    )(c_pad, w_ada, b_ada.reshape(DEPTH, 1, n))


def _norm_kernel(x_ref, g_ref, sc_ref, sh_ref, o_ref):
    x = x_ref[...]
    r = lax.rsqrt(jnp.mean(x * x, axis=-1, keepdims=True) + EPS)
    o_ref[...] = ((x * r * g_ref[0]) * (1.0 + sc_ref[0]) + sh_ref[0]).astype(o_ref.dtype)


def _mod_norm(x, gain, layer, scale, shift):
    tm = 512
    per_batch = SEQ // tm
    row = lambda i: (i // per_batch, 0, 0)
    return pl.pallas_call(
        _norm_kernel,
        grid=(ROWS // tm,),
        in_specs=[
            pl.BlockSpec((tm, D_MODEL), lambda i: (i, 0)),
            pl.BlockSpec((1, 1, D_MODEL), lambda i: (layer, 0, 0)),
            pl.BlockSpec((1, 1, D_MODEL), row),
            pl.BlockSpec((1, 1, D_MODEL), row),
        ],
        out_specs=pl.BlockSpec((tm, D_MODEL), lambda i: (i, 0)),
        out_shape=jax.ShapeDtypeStruct((ROWS, D_MODEL), BF16),
        compiler_params=_params("arbitrary"),
        name="mod_norm",
    )(x, gain.reshape(DEPTH, 1, D_MODEL), scale, shift)


def _final_norm_kernel(x_ref, g_ref, o_ref):
    x = x_ref[...]
    r = lax.rsqrt(jnp.mean(x * x, axis=-1, keepdims=True) + EPS)
    o_ref[...] = x * r * g_ref[...]


def _final_norm(x, gain):
    tm = 512
    return pl.pallas_call(
        _final_norm_kernel,
        grid=(ROWS // tm,),
        in_specs=[
            pl.BlockSpec((tm, D_MODEL), lambda i: (i, 0)),
            pl.BlockSpec((1, D_MODEL), lambda i: (0, 0)),
        ],
        out_specs=pl.BlockSpec((tm, D_MODEL), lambda i: (i, 0)),
        out_shape=jax.ShapeDtypeStruct((ROWS, D_MODEL), F32),
        compiler_params=_params("arbitrary"),
        name="final_norm",
    )(x, gain.reshape(1, D_MODEL))


def _cast_chunk(w_ref, scr, slot, step, col0=0):
    rows, cols = w_ref.shape[1], w_ref.shape[2]
    r0 = pl.multiple_of(step * rows, rows)
    scr[slot, pl.ds(r0, rows), col0:col0 + cols] = w_ref[0].astype(BF16)


def _ws_ids():
    j = pl.program_id(0)
    return j, pl.program_id(1), j % 2


def _w_spec(layer, rows, cols, n_tiles, col_tile0=0, row_chunk0=0):
    return pl.BlockSpec((1, rows, cols),
                        lambda j, i: (layer, row_chunk0 + i, col_tile0 + jnp.minimum(j, n_tiles - 1)))


def _a_spec(tm, k, col_block=0):
    return pl.BlockSpec((tm, k), lambda j, i: (jnp.where(j > 0, i, 0), col_block))


def _o_spec(tm, tn):
    return pl.BlockSpec((tm, tn), lambda j, i: (jnp.where(j > 0, i, 0), jnp.maximum(j - 1, 0)))


def _proj_kernel(a_ref, w_ref, o_ref, scr, *, scale):
    j, i, slot = _ws_ids()

    @pl.when(j == 0)
    def _():
        _cast_chunk(w_ref, scr, slot, i)

    @pl.when(j > 0)
    def _():
        acc = _dot(a_ref[...], scr[1 - slot])
        if scale != 1.0:
            acc = acc * scale
        o_ref[...] = acc.astype(o_ref.dtype)
        _cast_chunk(w_ref, scr, slot, i)


def _proj(a, w, layer, col0, n, out_dtype, scale=1.0, tm=1024, tn=512):
    m, k = a.shape
    ni, nj = m // tm, n // tn
    return pl.pallas_call(
        functools.partial(_proj_kernel, scale=scale),
        grid=(nj + 1, ni),
        in_specs=[_a_spec(tm, k), _w_spec(layer, k // ni, tn, nj, col_tile0=col0 // tn)],
        out_specs=_o_spec(tm, tn),
        out_shape=jax.ShapeDtypeStruct((m, n), out_dtype),
        scratch_shapes=[pltpu.VMEM((2, k, tn), BF16)],
        compiler_params=_params("arbitrary", "arbitrary"),
        name="proj",
    )(a, w)


def _pool_kernel(u_ref, w_ref, s_ref, o_ref, acc_ref):
    g = pl.program_id(1)
    x = u_ref[...]
    row = lax.broadcasted_iota(jnp.int32, (SEQ, 1), 0)

    def lagged(v, lag):
        return jnp.where(row >= lag, pltpu.roll(v, lag, axis=0), 0.0)

    acc_ref[...] = x + lagged(x, 1)
    for level in range(1, N_GROUPS):
        @pl.when(g >= level)
        def _():
            v = acc_ref[...]
            acc_ref[...] = v + lagged(v, 1 << level)

    window = jnp.left_shift(2, g)
    count = jnp.minimum(row + 1, window).astype(F32)
    pooled = acc_ref[...] / count - x
    y = _dot(pooled.astype(BF16), w_ref[0, 0].astype(BF16))
    o_ref[...] = (y * s_ref[0]).astype(o_ref.dtype)


def _pool_branch(u, w_mix, scale, layer):
    return pl.pallas_call(
        _pool_kernel,
        grid=(BATCH, N_GROUPS),
        in_specs=[
            pl.BlockSpec((SEQ, POOL_GROUP), lambda b, g: (b, g)),
            pl.BlockSpec((1, 1, POOL_GROUP, POOL_GROUP), lambda b, g: (layer, g, 0, 0)),
            pl.BlockSpec((1, 1, POOL_GROUP), lambda b, g: (layer, 0, g)),
        ],
        out_specs=pl.BlockSpec((SEQ, POOL_GROUP), lambda b, g: (b, g)),
        out_shape=jax.ShapeDtypeStruct((ROWS, POOL_WIDTH), BF16),
        scratch_shapes=[pltpu.VMEM((SEQ, POOL_GROUP), F32)],
        compiler_params=_params("arbitrary", "arbitrary"),
        name="pool_branch",
    )(u, w_mix, scale.reshape(DEPTH, 1, POOL_WIDTH))


def _bias_kernel(rel_ref, bucket_ref, o_ref):
    h = pl.program_id(1)
    bucket = bucket_ref[0]
    for p in range(Q_PER_KV // 2):
        for e in range(2):
            head = h * Q_PER_KV + 2 * p + e
            acc = jnp.full(bucket.shape, NEG, F32)
            for b in range(N_BUCKETS):
                acc = jnp.where(bucket == b, rel_ref[b, head], acc)
            o_ref[0, 0, p * BLOCK:(p + 1) * BLOCK, e * 2 * BLOCK:(e + 1) * 2 * BLOCK] = acc


def _band_bias(rel_bias, bucket_pair):
    pairs = Q_PER_KV // 2
    return pl.pallas_call(
        _bias_kernel,
        grid=(2, N_KV_HEADS),
        in_specs=[
            pl.BlockSpec(memory_space=pltpu.SMEM),
            pl.BlockSpec((1, BLOCK, 2 * BLOCK), lambda f, h: (f, 0, 0)),
        ],
        out_specs=pl.BlockSpec((1, 1, pairs * BLOCK, 4 * BLOCK), lambda f, h: (f, h, 0, 0)),
        out_shape=jax.ShapeDtypeStruct((2, N_KV_HEADS, pairs * BLOCK, 4 * BLOCK), F32),
        compiler_params=_params("arbitrary", "arbitrary"),
        name="band_bias",
    )(rel_bias, bucket_pair)


def _bucket_tables():
    i = jnp.arange(BLOCK)[:, None]
    j = jnp.arange(2 * BLOCK)[None, :]
    dist = i + BLOCK - j
    max_exact = N_BUCKETS // 2
    d = jnp.maximum(dist, 0)
    log_ratio = jnp.log(jnp.maximum(d, 1).astype(F32) / max_exact) / math.log(MAX_DISTANCE / max_exact)
    large = jnp.minimum(max_exact + (log_ratio * (N_BUCKETS - max_exact)).astype(jnp.int32), N_BUCKETS - 1)
    bucket = jnp.where(d < max_exact, d, large)
    in_window = (dist >= 0) & (dist < WINDOW)
    later = jnp.where(in_window, bucket, -1)
    first = jnp.where(j >= BLOCK, later, -1)
    return jnp.stack([first, later]).astype(jnp.int32)


def _attn_kernel(sink_ref, q_ref, kvp_ref, kvc_ref, bias_ref, o_ref, *, layer):
    pairs = Q_PER_KV // 2
    lane = lax.broadcasted_iota(jnp.int32, (1, 2 * HEAD_DIM), 1)
    low = lane < HEAD_DIM
    for h in range(N_KV_HEADS):
        slab = (h // 2) * 2 * HEAD_DIM
        k_band = jnp.concatenate([kvp_ref[:, slab:slab + 2 * HEAD_DIM],
                                  kvc_ref[:, slab:slab + 2 * HEAD_DIM]], axis=0)
        v_band = jnp.concatenate([kvp_ref[:, KV_WIDTH + slab:KV_WIDTH + slab + 2 * HEAD_DIM],
                                  kvc_ref[:, KV_WIDTH + slab:KV_WIDTH + slab + 2 * HEAD_DIM]], axis=0)
        k_swap = pltpu.roll(k_band, HEAD_DIM, axis=1)
        v_swap = pltpu.roll(v_band, HEAD_DIM, axis=1)
        k_low, k_high = (k_band, k_swap) if h % 2 == 0 else (k_swap, k_band)
        v_low, v_high = (v_band, v_swap) if h % 2 == 0 else (v_swap, v_band)
        zero = jnp.zeros_like(v_band)
        v_diag = jnp.concatenate([jnp.where(low, v_low, zero), jnp.where(low, zero, v_high)], axis=0)

        q4 = jnp.concatenate(
            [q_ref[:, h * Q_PER_KV * HEAD_DIM + p * 2 * HEAD_DIM:
                   h * Q_PER_KV * HEAD_DIM + (p + 1) * 2 * HEAD_DIM] for p in range(pairs)], axis=0)
        qz = jnp.zeros_like(q4)
        dims = (((1,), (1,)), ((), ()))
        logit_even = lax.dot_general(jnp.where(low, q4, qz), k_low, dims, preferred_element_type=F32)
        logit_odd = lax.dot_general(jnp.where(low, qz, q4), k_high, dims, preferred_element_type=F32)
        logits = jnp.concatenate([logit_even, logit_odd], axis=1) + bias_ref[0, h]

        prob_rows = []
        for p in range(pairs):
            halves = []
            for e in range(2):
                sink = sink_ref[layer, h * Q_PER_KV + 2 * p + e]
                blk = logits[p * BLOCK:(p + 1) * BLOCK, e * 2 * BLOCK:(e + 1) * 2 * BLOCK]
                m = jnp.maximum(jnp.max(blk, axis=1, keepdims=True), sink)
                ex = jnp.exp(blk - m)
                denom = jnp.sum(ex, axis=1, keepdims=True) + jnp.exp(sink - m)
                halves.append((ex * (1.0 / denom)).astype(BF16))
            prob_rows.append(jnp.concatenate(halves, axis=1))
        probs = jnp.concatenate(prob_rows, axis=0)
        out = _dot(probs, v_diag)
        for p in range(pairs):
            c0 = h * Q_PER_KV * HEAD_DIM + p * 2 * HEAD_DIM
            o_ref[:, c0:c0 + 2 * HEAD_DIM] = out[p * BLOCK:(p + 1) * BLOCK].astype(o_ref.dtype)


def _attention(q, kv, sinks, bias, layer):
    n_steps = ROWS // BLOCK
    return pl.pallas_call(
        functools.partial(_attn_kernel, layer=layer),
        grid=(n_steps,),
        in_specs=[
            pl.BlockSpec(memory_space=pltpu.SMEM),
            pl.BlockSpec((BLOCK, Q_WIDTH), lambda r: (r, 0)),
            pl.BlockSpec((BLOCK, 2 * KV_WIDTH), lambda r: (jnp.maximum(r - 1, 0), 0)),
            pl.BlockSpec((BLOCK, 2 * KV_WIDTH), lambda r: (r, 0)),
            pl.BlockSpec((1, N_KV_HEADS, (Q_PER_KV // 2) * BLOCK, 4 * BLOCK),
                         lambda r: (jnp.minimum(r % N_BLOCKS, 1), 0, 0, 0)),
        ],
        out_specs=pl.BlockSpec((BLOCK, Q_WIDTH), lambda r: (r, 0)),
        out_shape=jax.ShapeDtypeStruct((ROWS, Q_WIDTH), BF16),
        compiler_params=_params("arbitrary"),
        name="swa_attention",
    )(sinks, q, kv, kv, bias)


def _merge_kernel(h_ref, yp_ref, ya_ref, wgp_ref, wga_ref, wbp_ref, wba_ref, o_ref,
                  sgp, sga, sbp, sba):
    j, i, slot = _ws_ids()

    def cast():
        _cast_chunk(wgp_ref, sgp, slot, i)
        _cast_chunk(wga_ref, sga, slot, i)
        _cast_chunk(wbp_ref, sbp, slot, i)
        _cast_chunk(wba_ref, sba, slot, i)

    @pl.when(j == 0)
    def _():
        cast()

    @pl.when(j > 0)
    def _():
        h = h_ref[...]
        gate_pool = jax.nn.sigmoid(_dot(h, sgp[1 - slot]))
        gate_attn = jax.nn.sigmoid(_dot(h, sga[1 - slot]))
        y_pool = _dot(yp_ref[...], sbp[1 - slot])
        y_attn = _dot(ya_ref[...], sba[1 - slot])
        o_ref[...] = (gate_pool * y_pool + gate_attn * y_attn).astype(o_ref.dtype)
        cast()


def _merge(h, yp, ya, w_gate, w_bp, w_ba, layer, tm=512, tn=512):
    ni, nj = ROWS // tm, D_MODEL // tn
    return pl.pallas_call(
        _merge_kernel,
        grid=(nj + 1, ni),
        in_specs=[
            _a_spec(tm, D_MODEL), _a_spec(tm, POOL_WIDTH), _a_spec(tm, Q_WIDTH),
            _w_spec(layer, D_MODEL // ni, tn, nj),
            _w_spec(layer, D_MODEL // ni, tn, nj, col_tile0=nj),
            _w_spec(layer, POOL_WIDTH // ni, tn, nj),
            _w_spec(layer, Q_WIDTH // ni, tn, nj),
        ],
        out_specs=_o_spec(tm, tn),
        out_shape=jax.ShapeDtypeStruct((ROWS, D_MODEL), BF16),
        scratch_shapes=[pltpu.VMEM((2, D_MODEL, tn), BF16), pltpu.VMEM((2, D_MODEL, tn), BF16),
                        pltpu.VMEM((2, POOL_WIDTH, tn), BF16), pltpu.VMEM((2, Q_WIDTH, tn), BF16)],
        compiler_params=_params("arbitrary", "arbitrary"),
        name="gated_merge",
    )(h, yp, ya, w_gate, w_gate, w_bp, w_ba)


def _resid_kernel(a_ref, w_ref, x_ref, g_ref, o_ref, scr):
    j, i, slot = _ws_ids()

    @pl.when(j == 0)
    def _():
        _cast_chunk(w_ref, scr, slot, i)

    @pl.when(j > 0)
    def _():
        o_ref[...] = x_ref[...] + g_ref[0] * _dot(a_ref[...], scr[1 - slot])
        _cast_chunk(w_ref, scr, slot, i)


def _resid_proj(a, w, x, gate, layer, tm, tn):
    m, k = a.shape
    n = x.shape[1]
    ni, nj = m // tm, n // tn
    per_batch = SEQ // tm
    return pl.pallas_call(
        _resid_kernel,
        grid=(nj + 1, ni),
        in_specs=[
            _a_spec(tm, k),
            _w_spec(layer, k // ni, tn, nj),
            _o_spec(tm, tn),
            pl.BlockSpec((1, 1, tn), lambda j, i: (i // per_batch, 0, jnp.maximum(j - 1, 0))),
        ],
        out_specs=_o_spec(tm, tn),
        out_shape=jax.ShapeDtypeStruct((m, n), F32),
        scratch_shapes=[pltpu.VMEM((2, k, tn), BF16)],
        compiler_params=_params("arbitrary", "arbitrary"),
        name="resid_proj",
    )(a, w, x, gate)


FF_TILE = V7X_MXU_DIM


def _swiglu_kernel(a_ref, wa_ref, wb_ref, o_ref, scr):
    j, i, slot = _ws_ids()

    def cast():
        _cast_chunk(wa_ref, scr, slot, i)
        _cast_chunk(wb_ref, scr, slot, i, col0=FF_TILE)

    @pl.when(j == 0)
    def _():
        cast()

    @pl.when(j > 0)
    def _():
        ab = _dot(a_ref[...], scr[1 - slot])
        a = ab[:, :FF_TILE]
        b = ab[:, FF_TILE:]
        o_ref[...] = ((a * jax.nn.sigmoid(a)) * b).astype(o_ref.dtype)
        cast()


def _swiglu(h2, w_ffn_in, layer, tm=2048):
    ni, nj = ROWS // tm, D_FF // FF_TILE
    return pl.pallas_call(
        _swiglu_kernel,
        grid=(nj + 1, ni),
        in_specs=[
            _a_spec(tm, D_MODEL),
            _w_spec(layer, D_MODEL // ni, FF_TILE, nj),
            _w_spec(layer, D_MODEL // ni, FF_TILE, nj, col_tile0=nj),
        ],
        out_specs=_o_spec(tm, FF_TILE),
        out_shape=jax.ShapeDtypeStruct((ROWS, D_FF), BF16),
        scratch_shapes=[pltpu.VMEM((2, D_MODEL, 2 * FF_TILE), BF16)],
        compiler_params=_params("arbitrary", "arbitrary"),
        name="swiglu_up",
    )(h2, w_ffn_in, w_ffn_in)


def kernel(x, c, w_ada, b_ada, norm1, w_in, w_pool_mix, pool_scale, sinks, rel_bias,
           w_branch_pool, w_branch_attn, w_gate, w_out, norm2, w_ffn_in, w_ffn_out,
           final_norm):
    c_pad = jnp.pad(c, ((0, 8 - BATCH), (0, 0)))
    mod = _ada(c_pad, w_ada, b_ada)[:, :BATCH]
    bias = _band_bias(rel_bias, _bucket_tables())
    xr = x.reshape(ROWS, D_MODEL)

    for l in range(DEPTH):
        sh1, sc1, g1, sh2, sc2, g2 = [m.reshape(BATCH, 1, D_MODEL)
                                      for m in jnp.split(mod[l], N_MOD, axis=-1)]
        h = _mod_norm(xr, norm1, l, sc1, sh1)
        u = _proj(h, w_in, l, 0, POOL_WIDTH, F32)
        q = _proj(h, w_in, l, POOL_WIDTH, Q_WIDTH, BF16, scale=HEAD_DIM ** -0.5)
        kv = _proj(h, w_in, l, POOL_WIDTH + Q_WIDTH, 2 * KV_WIDTH, BF16)
        yp = _pool_branch(u, w_pool_mix, pool_scale, l)
        ya = _attention(q, kv, sinks, bias, l)
        merged = _merge(h, yp, ya, w_gate, w_branch_pool, w_branch_attn, l)
        xr = _resid_proj(merged, w_out, xr, g1, l, tm=1024, tn=512)

        h2 = _mod_norm(xr, norm2, l, sc2, sh2)
        act = _swiglu(h2, w_ffn_in, l)
        xr = _resid_proj(act, w_ffn_out, xr, g2, l, tm=512, tn=512)

    return _final_norm(xr, final_norm).reshape(BATCH, SEQ, D_MODEL)
```

```python
import functools
import math

import jax
import jax.numpy as jnp
from jax import lax
from jax.experimental import pallas as pl
from jax.experimental.pallas import tpu as pltpu

D_MODEL = 4096
BATCH = 4
SEQ = 2048
ROWS = BATCH * SEQ
DEPTH = 2
POOL_WINDOWS = (2, 4, 8, 16)
N_GROUPS = len(POOL_WINDOWS)
POOL_WIDTH = D_MODEL // 2
POOL_GROUP = POOL_WIDTH // N_GROUPS
HEAD_DIM = 64
N_Q_HEADS = (D_MODEL // 2) // HEAD_DIM
N_KV_HEADS = N_Q_HEADS // 8
Q_PER_KV = N_Q_HEADS // N_KV_HEADS
Q_WIDTH = N_Q_HEADS * HEAD_DIM
KV_WIDTH = N_KV_HEADS * HEAD_DIM
WINDOW = 128
BLOCK = 128
N_BLOCKS = SEQ // BLOCK
N_BUCKETS = 32
MAX_DISTANCE = 128
IN_WIDTH = POOL_WIDTH + Q_WIDTH + 2 * KV_WIDTH
D_FF = -(-(8 * D_MODEL) // (3 * 256)) * 256
N_MOD = 6
EPS = 1e-6

V7X_VMEM_BYTES = 64 * 1024 * 1024
VMEM_LIMIT = V7X_VMEM_BYTES - 8 * 1024 * 1024
V7X_MXU_DIM = 256

NEG = -1e30

BF16 = jnp.bfloat16
F32 = jnp.float32


def _params(*sem):
    return pltpu.CompilerParams(dimension_semantics=sem, vmem_limit_bytes=VMEM_LIMIT)


def _dot(a, b):
    return jnp.dot(a, b, preferred_element_type=F32)


ADA_ROWS = 128


def _ada_kernel(c_ref, w_ref, b_ref, o_ref):
    k = pl.program_id(1)

    @pl.when(k == 0)
    def _():
        o_ref[0] = jnp.broadcast_to(b_ref[0], o_ref.shape[1:])

    c = c_ref[...]
    s = (c * jax.nn.sigmoid(c)).astype(BF16)
    o_ref[0] += _dot(s, w_ref[0].astype(BF16))


def _ada(c_pad, w_ada, b_ada):
    n = N_MOD * D_MODEL
    return pl.pallas_call(
        _ada_kernel,
        grid=(DEPTH, D_MODEL // ADA_ROWS),
        in_specs=[
            pl.BlockSpec((8, ADA_ROWS), lambda l, k: (0, k)),
            pl.BlockSpec((1, ADA_ROWS, n), lambda l, k: (l, k, 0)),
            pl.BlockSpec((1, 1, n), lambda l, k: (l, 0, 0)),
        ],
        out_specs=pl.BlockSpec((1, 8, n), lambda l, k: (l, 0, 0)),
        out_shape=jax.ShapeDtypeStruct((DEPTH, 8, n), F32),
        compiler_params=_params("arbitrary", "arbitrary"),
        name="ada_mod",
    )(c_pad, w_ada, b_ada.reshape(DEPTH, 1, n))


def _norm_kernel(x_ref, g_ref, sc_ref, sh_ref, o_ref):
    x = x_ref[...]
    r = lax.rsqrt(jnp.mean(x * x, axis=-1, keepdims=True) + EPS)
    o_ref[...] = ((x * r * g_ref[0]) * (1.0 + sc_ref[0]) + sh_ref[0]).astype(o_ref.dtype)


def _mod_norm(x, gain, layer, scale, shift):
    tm = 512
    per_batch = SEQ // tm
    row = lambda i: (i // per_batch, 0, 0)
    return pl.pallas_call(
        _norm_kernel,
        grid=(ROWS // tm,),
        in_specs=[
            pl.BlockSpec((tm, D_MODEL), lambda i: (i, 0)),
            pl.BlockSpec((1, 1, D_MODEL), lambda i: (layer, 0, 0)),
            pl.BlockSpec((1, 1, D_MODEL), row),
            pl.BlockSpec((1, 1, D_MODEL), row),
        ],
        out_specs=pl.BlockSpec((tm, D_MODEL), lambda i: (i, 0)),
        out_shape=jax.ShapeDtypeStruct((ROWS, D_MODEL), BF16),
        compiler_params=_params("arbitrary"),
        name="mod_norm",
    )(x, gain.reshape(DEPTH, 1, D_MODEL), scale, shift)


def _final_norm_kernel(x_ref, g_ref, o_ref):
    x = x_ref[...]
    r = lax.rsqrt(jnp.mean(x * x, axis=-1, keepdims=True) + EPS)
    o_ref[...] = x * r * g_ref[...]


def _final_norm(x, gain):
    tm = 512
    return pl.pallas_call(
        _final_norm_kernel,
        grid=(ROWS // tm,),
        in_specs=[
            pl.BlockSpec((tm, D_MODEL), lambda i: (i, 0)),
            pl.BlockSpec((1, D_MODEL), lambda i: (0, 0)),
        ],
        out_specs=pl.BlockSpec((tm, D_MODEL), lambda i: (i, 0)),
        out_shape=jax.ShapeDtypeStruct((ROWS, D_MODEL), F32),
        compiler_params=_params("arbitrary"),
        name="final_norm",
    )(x, gain.reshape(1, D_MODEL))


def _cast_chunk(w_ref, scr, slot, step, col0=0):
    rows, cols = w_ref.shape[1], w_ref.shape[2]
    r0 = pl.multiple_of(step * rows, rows)
    scr[slot, pl.ds(r0, rows), col0:col0 + cols] = w_ref[0].astype(BF16)


def _ws_ids():
    j = pl.program_id(0)
    return j, pl.program_id(1), j % 2


def _w_spec(layer, rows, cols, n_tiles, col_tile0=0, row_chunk0=0):
    return pl.BlockSpec((1, rows, cols),
                        lambda j, i: (layer, row_chunk0 + i, col_tile0 + jnp.minimum(j, n_tiles - 1)))


def _a_spec(tm, k, col_block=0):
    return pl.BlockSpec((tm, k), lambda j, i: (jnp.where(j > 0, i, 0), col_block))


def _o_spec(tm, tn):
    return pl.BlockSpec((tm, tn), lambda j, i: (jnp.where(j > 0, i, 0), jnp.maximum(j - 1, 0)))


def _proj_kernel(a_ref, w_ref, o_ref, scr, *, scale):
    j, i, slot = _ws_ids()

    @pl.when(j == 0)
    def _():
        _cast_chunk(w_ref, scr, slot, i)

    @pl.when(j > 0)
    def _():
        acc = _dot(a_ref[...], scr[1 - slot])
        if scale != 1.0:
            acc = acc * scale
        o_ref[...] = acc.astype(o_ref.dtype)
        _cast_chunk(w_ref, scr, slot, i)


def _proj(a, w, layer, col0, n, out_dtype, scale=1.0, tm=1024, tn=512):
    m, k = a.shape
    ni, nj = m // tm, n // tn
    return pl.pallas_call(
        functools.partial(_proj_kernel, scale=scale),
        grid=(nj + 1, ni),
        in_specs=[_a_spec(tm, k), _w_spec(layer, k // ni, tn, nj, col_tile0=col0 // tn)],
        out_specs=_o_spec(tm, tn),
        out_shape=jax.ShapeDtypeStruct((m, n), out_dtype),
        scratch_shapes=[pltpu.VMEM((2, k, tn), BF16)],
        compiler_params=_params("arbitrary", "arbitrary"),
        name="proj",
    )(a, w)


POOL_PAD = 16


def _pool_kernel(u_ref, w_ref, s_ref, o_ref, a_ref, b_ref):
    g = pl.program_id(1)
    zeros = jnp.zeros((POOL_PAD, POOL_GROUP), F32)
    a_ref[0:POOL_PAD, :] = zeros
    b_ref[0:POOL_PAD, :] = zeros
    body = pl.ds(POOL_PAD, SEQ)
    a_ref[body, :] = u_ref[...]
    row = lax.broadcasted_iota(jnp.int32, (SEQ, 1), 0)

    def lagged(ref, lag):
        return ref[pl.ds(POOL_PAD - lag, SEQ), :]

    for group in range(N_GROUPS):
        @pl.when(g == group)
        def _():
            src, dst = a_ref, b_ref
            for level in range(group):
                dst[body, :] = src[body, :] + lagged(src, 1 << level)
                src, dst = dst, src
            total = src[body, :] + lagged(src, 1 << group)
            inv_count = 1.0 / jnp.minimum(row + 1, POOL_WINDOWS[group]).astype(F32)
            pooled = total * inv_count - u_ref[...]
            y = _dot(pooled.astype(BF16), w_ref[0, 0].astype(BF16))
            o_ref[...] = (y * s_ref[0]).astype(o_ref.dtype)


def _pool_branch(u, w_mix, scale, layer):
    return pl.pallas_call(
        _pool_kernel,
        grid=(BATCH, N_GROUPS),
        in_specs=[
            pl.BlockSpec((SEQ, POOL_GROUP), lambda b, g: (b, g)),
            pl.BlockSpec((1, 1, POOL_GROUP, POOL_GROUP), lambda b, g: (layer, g, 0, 0)),
            pl.BlockSpec((1, 1, POOL_GROUP), lambda b, g: (layer, 0, g)),
        ],
        out_specs=pl.BlockSpec((SEQ, POOL_GROUP), lambda b, g: (b, g)),
        out_shape=jax.ShapeDtypeStruct((ROWS, POOL_WIDTH), BF16),
        scratch_shapes=[pltpu.VMEM((POOL_PAD + SEQ, POOL_GROUP), F32),
                        pltpu.VMEM((POOL_PAD + SEQ, POOL_GROUP), F32)],
        compiler_params=_params("arbitrary", "arbitrary"),
        name="pool_branch",
    )(u, w_mix, scale.reshape(DEPTH, 1, POOL_WIDTH))


BAND = 2 * BLOCK
PAIRS = Q_PER_KV // 2


def _bias_kernel(rel_ref, bucket_ref, o_ref):
    h = pl.program_id(1)
    bucket = bucket_ref[0]
    for p in range(PAIRS):
        for e in range(2):
            head = h * Q_PER_KV + 2 * p + e
            acc = jnp.full(bucket.shape, NEG, F32)
            for b in range(N_BUCKETS):
                acc = jnp.where(bucket == b, rel_ref[b, head], acc)
            r0 = ((p // 2) * 2 + e) * BAND
            c0 = (p % 2) * BLOCK
            o_ref[0, 0, r0:r0 + BAND, c0:c0 + BLOCK] = acc


def _band_bias(rel_bias, bucket_pair):
    return pl.pallas_call(
        _bias_kernel,
        grid=(2, N_KV_HEADS),
        in_specs=[
            pl.BlockSpec(memory_space=pltpu.SMEM),
            pl.BlockSpec((1, BAND, BLOCK), lambda f, h: (f, 0, 0)),
        ],
        out_specs=pl.BlockSpec((1, 1, PAIRS * BAND, 2 * BLOCK), lambda f, h: (f, h, 0, 0)),
        out_shape=jax.ShapeDtypeStruct((2, N_KV_HEADS, PAIRS * BAND, 2 * BLOCK), F32),
        compiler_params=_params("arbitrary", "arbitrary"),
        name="band_bias",
    )(rel_bias, bucket_pair)


def _bucket_tables():
    i = jnp.arange(BLOCK)[None, :]
    j = jnp.arange(BAND)[:, None]
    dist = i + BLOCK - j
    max_exact = N_BUCKETS // 2
    d = jnp.maximum(dist, 0)
    log_ratio = jnp.log(jnp.maximum(d, 1).astype(F32) / max_exact) / math.log(MAX_DISTANCE / max_exact)
    large = jnp.minimum(max_exact + (log_ratio * (N_BUCKETS - max_exact)).astype(jnp.int32), N_BUCKETS - 1)
    bucket = jnp.where(d < max_exact, d, large)
    in_window = (dist >= 0) & (dist < WINDOW)
    later = jnp.where(in_window, bucket, -1)
    first = jnp.where(j >= BLOCK, later, -1)
    return jnp.stack([first, later]).astype(jnp.int32)


def _attn_kernel(sink_ref, q_ref, kvp_ref, kvc_ref, bias_ref, o_ref, *, layer):
    lane = lax.broadcasted_iota(jnp.int32, (1, 2 * HEAD_DIM), 1)
    low = lane < HEAD_DIM
    left = lax.broadcasted_iota(jnp.int32, (1, 2 * BLOCK), 1) < BLOCK
    ones = jnp.ones((16, BAND), BF16)
    nt = (((1,), (1,)), ((), ()))
    for slab in range(N_KV_HEADS // 2):
        c = slab * 2 * HEAD_DIM
        k_slab = jnp.concatenate([kvp_ref[:, c:c + 2 * HEAD_DIM], kvc_ref[:, c:c + 2 * HEAD_DIM]], axis=0)
        v_slab = jnp.concatenate([kvp_ref[:, KV_WIDTH + c:KV_WIDTH + c + 2 * HEAD_DIM],
                                  kvc_ref[:, KV_WIDTH + c:KV_WIDTH + c + 2 * HEAD_DIM]], axis=0)
        k_swap = pltpu.roll(k_slab, HEAD_DIM, axis=1)
        v_t = v_slab.astype(F32).T.astype(BF16)
        zero = jnp.zeros_like(k_slab)
        for hh in range(2):
            h = 2 * slab + hh
            k_on_low, k_on_high = (k_slab, k_swap) if hh == 0 else (k_swap, k_slab)
            k_par = (jnp.where(low, k_on_low, zero), jnp.where(low, zero, k_on_high))
            v_rows = jnp.concatenate([v_t[hh * HEAD_DIM:(hh + 1) * HEAD_DIM], ones], axis=0)
            for t in range(PAIRS // 2):
                c0 = h * Q_PER_KV * HEAD_DIM + t * 2 * BLOCK
                q2 = jnp.concatenate([q_ref[:, c0:c0 + BLOCK], q_ref[:, c0 + BLOCK:c0 + 2 * BLOCK]], axis=0)
                probs, tails = [], []
                for e in range(2):
                    r0 = (t * 2 + e) * BAND
                    s_t = (lax.dot_general(k_par[e], q2, nt, preferred_element_type=F32)
                           + bias_ref[0, h, r0:r0 + BAND, :])
                    head0 = h * Q_PER_KV + 4 * t + e
                    sink = jnp.where(left, sink_ref[layer, head0], sink_ref[layer, head0 + 2])
                    m = jnp.maximum(jnp.max(s_t, axis=0, keepdims=True), sink)
                    probs.append(jnp.exp(s_t - m).astype(BF16))
                    tails.append(jnp.exp(sink - m))
                for pp in range(2):
                    cols = slice(pp * BLOCK, (pp + 1) * BLOCK)
                    p_t = jnp.concatenate([probs[0][:, cols], probs[1][:, cols]], axis=1)
                    tail = jnp.concatenate([tails[0][:, cols], tails[1][:, cols]], axis=1)
                    res = _dot(v_rows, p_t)
                    out_t = res[:HEAD_DIM] * (1.0 / (res[HEAD_DIM:HEAD_DIM + 1] + tail))
                    pair_t = jnp.concatenate([out_t[:, :BLOCK], out_t[:, BLOCK:]], axis=0)
                    o0 = c0 + pp * BLOCK
                    o_ref[:, o0:o0 + BLOCK] = pair_t.T.astype(o_ref.dtype)


def _attention(q, kv, sinks, bias, layer):
    n_steps = ROWS // BLOCK
    return pl.pallas_call(
        functools.partial(_attn_kernel, layer=layer),
        grid=(n_steps,),
        in_specs=[
            pl.BlockSpec(memory_space=pltpu.SMEM),
            pl.BlockSpec((BLOCK, Q_WIDTH), lambda r: (r, 0)),
            pl.BlockSpec((BLOCK, 2 * KV_WIDTH), lambda r: (jnp.maximum(r - 1, 0), 0)),
            pl.BlockSpec((BLOCK, 2 * KV_WIDTH), lambda r: (r, 0)),
            pl.BlockSpec((1, N_KV_HEADS, PAIRS * BAND, 2 * BLOCK),
                         lambda r: (jnp.minimum(r % N_BLOCKS, 1), 0, 0, 0)),
        ],
        out_specs=pl.BlockSpec((BLOCK, Q_WIDTH), lambda r: (r, 0)),
        out_shape=jax.ShapeDtypeStruct((ROWS, Q_WIDTH), BF16),
        compiler_params=_params("arbitrary"),
        name="swa_attention",
    )(sinks, q, kv, kv, bias)


def _merge_kernel(h_ref, yp_ref, ya_ref, wgp_ref, wga_ref, wbp_ref, wba_ref, o_ref,
                  sgp, sga, sbp, sba):
    j, i, slot = _ws_ids()

    def cast():
        _cast_chunk(wgp_ref, sgp, slot, i)
        _cast_chunk(wga_ref, sga, slot, i)
        _cast_chunk(wbp_ref, sbp, slot, i)
        _cast_chunk(wba_ref, sba, slot, i)

    @pl.when(j == 0)
    def _():
        cast()

    @pl.when(j > 0)
    def _():
        h = h_ref[...]
        gate_pool = jax.nn.sigmoid(_dot(h, sgp[1 - slot]))
        gate_attn = jax.nn.sigmoid(_dot(h, sga[1 - slot]))
        y_pool = _dot(yp_ref[...], sbp[1 - slot])
        y_attn = _dot(ya_ref[...], sba[1 - slot])
        o_ref[...] = (gate_pool * y_pool + gate_attn * y_attn).astype(o_ref.dtype)
        cast()


def _merge(h, yp, ya, w_gate, w_bp, w_ba, layer, tm=512, tn=512):
    ni, nj = ROWS // tm, D_MODEL // tn
    return pl.pallas_call(
        _merge_kernel,
        grid=(nj + 1, ni),
        in_specs=[
            _a_spec(tm, D_MODEL), _a_spec(tm, POOL_WIDTH), _a_spec(tm, Q_WIDTH),
            _w_spec(layer, D_MODEL // ni, tn, nj),
            _w_spec(layer, D_MODEL // ni, tn, nj, col_tile0=nj),
            _w_spec(layer, POOL_WIDTH // ni, tn, nj),
            _w_spec(layer, Q_WIDTH // ni, tn, nj),
        ],
        out_specs=_o_spec(tm, tn),
        out_shape=jax.ShapeDtypeStruct((ROWS, D_MODEL), BF16),
        scratch_shapes=[pltpu.VMEM((2, D_MODEL, tn), BF16), pltpu.VMEM((2, D_MODEL, tn), BF16),
                        pltpu.VMEM((2, POOL_WIDTH, tn), BF16), pltpu.VMEM((2, Q_WIDTH, tn), BF16)],
        compiler_params=_params("arbitrary", "arbitrary"),
        name="gated_merge",
    )(h, yp, ya, w_gate, w_gate, w_bp, w_ba)


def _resid_kernel(a_ref, w_ref, x_ref, g_ref, o_ref, scr):
    j, i, slot = _ws_ids()

    @pl.when(j == 0)
    def _():
        _cast_chunk(w_ref, scr, slot, i)

    @pl.when(j > 0)
    def _():
        o_ref[...] = x_ref[...] + g_ref[0] * _dot(a_ref[...], scr[1 - slot])
        _cast_chunk(w_ref, scr, slot, i)


def _resid_proj(a, w, x, gate, layer, tm, tn):
    m, k = a.shape
    n = x.shape[1]
    ni, nj = m // tm, n // tn
    per_batch = SEQ // tm
    return pl.pallas_call(
        _resid_kernel,
        grid=(nj + 1, ni),
        in_specs=[
            _a_spec(tm, k),
            _w_spec(layer, k // ni, tn, nj),
            _o_spec(tm, tn),
            pl.BlockSpec((1, 1, tn), lambda j, i: (i // per_batch, 0, jnp.maximum(j - 1, 0))),
        ],
        out_specs=_o_spec(tm, tn),
        out_shape=jax.ShapeDtypeStruct((m, n), F32),
        scratch_shapes=[pltpu.VMEM((2, k, tn), BF16)],
        compiler_params=_params("arbitrary", "arbitrary"),
        name="resid_proj",
    )(a, w, x, gate)


FF_TILE = V7X_MXU_DIM


def _swiglu_kernel(a_ref, wa_ref, wb_ref, o_ref, scr):
    j, i, slot = _ws_ids()

    def cast():
        _cast_chunk(wa_ref, scr, slot, i)
        _cast_chunk(wb_ref, scr, slot, i, col0=FF_TILE)

    @pl.when(j == 0)
    def _():
        cast()

    @pl.when(j > 0)
    def _():
        ab = _dot(a_ref[...], scr[1 - slot])
        a = ab[:, :FF_TILE]
        b = ab[:, FF_TILE:]
        o_ref[...] = ((a * jax.nn.sigmoid(a)) * b).astype(o_ref.dtype)
        cast()


def _swiglu(h2, w_ffn_in, layer, tm=2048):
    ni, nj = ROWS // tm, D_FF // FF_TILE
    return pl.pallas_call(
        _swiglu_kernel,
        grid=(nj + 1, ni),
        in_specs=[
            _a_spec(tm, D_MODEL),
            _w_spec(layer, D_MODEL // ni, FF_TILE, nj),
            _w_spec(layer, D_MODEL // ni, FF_TILE, nj, col_tile0=nj),
        ],
        out_specs=_o_spec(tm, FF_TILE),
        out_shape=jax.ShapeDtypeStruct((ROWS, D_FF), BF16),
        scratch_shapes=[pltpu.VMEM((2, D_MODEL, 2 * FF_TILE), BF16)],
        compiler_params=_params("arbitrary", "arbitrary"),
        name="swiglu_up",
    )(h2, w_ffn_in, w_ffn_in)


def kernel(x, c, w_ada, b_ada, norm1, w_in, w_pool_mix, pool_scale, sinks, rel_bias,
           w_branch_pool, w_branch_attn, w_gate, w_out, norm2, w_ffn_in, w_ffn_out,
           final_norm):
    c_pad = jnp.pad(c, ((0, 8 - BATCH), (0, 0)))
    mod = _ada(c_pad, w_ada, b_ada)[:, :BATCH]
    bias = _band_bias(rel_bias, _bucket_tables())
    xr = x.reshape(ROWS, D_MODEL)

    for l in range(DEPTH):
        sh1, sc1, g1, sh2, sc2, g2 = [m.reshape(BATCH, 1, D_MODEL)
                                      for m in jnp.split(mod[l], N_MOD, axis=-1)]
        h = _mod_norm(xr, norm1, l, sc1, sh1)
        u = _proj(h, w_in, l, 0, POOL_WIDTH, F32, tm=512, tn=1024)
        q = _proj(h, w_in, l, POOL_WIDTH, Q_WIDTH, BF16, scale=HEAD_DIM ** -0.5, tm=512, tn=1024)
        kv = _proj(h, w_in, l, POOL_WIDTH + Q_WIDTH, 2 * KV_WIDTH, BF16)
        yp = _pool_branch(u, w_pool_mix, pool_scale, l)
        ya = _attention(q, kv, sinks, bias, l)
        merged = _merge(h, yp, ya, w_gate, w_branch_pool, w_branch_attn, l)
        xr = _resid_proj(merged, w_out, xr, g1, l, tm=512, tn=1024)

        h2 = _mod_norm(xr, norm2, l, sc2, sh2)
        act = _swiglu(h2, w_ffn_in, l)
        xr = _resid_proj(act, w_ffn_out, xr, g2, l, tm=512, tn=512)

    return _final_norm(xr, final_norm).reshape(BATCH, SEQ, D_MODEL)
```

```python
import functools
import math

import jax
import jax.numpy as jnp
from jax import lax
from jax.experimental import pallas as pl
from jax.experimental.pallas import tpu as pltpu

D_MODEL = 4096
BATCH = 4
SEQ = 2048
ROWS = BATCH * SEQ
DEPTH = 2
POOL_WINDOWS = (2, 4, 8, 16)
N_GROUPS = len(POOL_WINDOWS)
POOL_WIDTH = D_MODEL // 2
POOL_GROUP = POOL_WIDTH // N_GROUPS
HEAD_DIM = 64
N_Q_HEADS = (D_MODEL // 2) // HEAD_DIM
N_KV_HEADS = N_Q_HEADS // 8
Q_PER_KV = N_Q_HEADS // N_KV_HEADS
Q_WIDTH = N_Q_HEADS * HEAD_DIM
KV_WIDTH = N_KV_HEADS * HEAD_DIM
WINDOW = 128
BLOCK = 128
N_BLOCKS = SEQ // BLOCK
N_BUCKETS = 32
MAX_DISTANCE = 128
IN_WIDTH = POOL_WIDTH + Q_WIDTH + 2 * KV_WIDTH
D_FF = -(-(8 * D_MODEL) // (3 * 256)) * 256
N_MOD = 6
EPS = 1e-6

V7X_VMEM_BYTES = 64 * 1024 * 1024
VMEM_LIMIT = V7X_VMEM_BYTES - 8 * 1024 * 1024
V7X_MXU_DIM = 256

NEG = -1e30

BF16 = jnp.bfloat16
F32 = jnp.float32


def _params(*sem):
    return pltpu.CompilerParams(dimension_semantics=sem, vmem_limit_bytes=VMEM_LIMIT)


def _dot(a, b):
    return jnp.dot(a, b, preferred_element_type=F32)


ADA_ROWS = 128


def _ada_kernel(c_ref, w_ref, b_ref, o_ref):
    k = pl.program_id(1)

    @pl.when(k == 0)
    def _():
        o_ref[0] = jnp.broadcast_to(b_ref[0], o_ref.shape[1:])

    c = c_ref[...]
    s = (c * jax.nn.sigmoid(c)).astype(BF16)
    o_ref[0] += _dot(s, w_ref[0].astype(BF16))


def _ada(c_pad, w_ada, b_ada):
    n = N_MOD * D_MODEL
    return pl.pallas_call(
        _ada_kernel,
        grid=(DEPTH, D_MODEL // ADA_ROWS),
        in_specs=[
            pl.BlockSpec((8, ADA_ROWS), lambda l, k: (0, k)),
            pl.BlockSpec((1, ADA_ROWS, n), lambda l, k: (l, k, 0)),
            pl.BlockSpec((1, 1, n), lambda l, k: (l, 0, 0)),
        ],
        out_specs=pl.BlockSpec((1, 8, n), lambda l, k: (l, 0, 0)),
        out_shape=jax.ShapeDtypeStruct((DEPTH, 8, n), F32),
        compiler_params=_params("arbitrary", "arbitrary"),
        name="ada_mod",
    )(c_pad, w_ada, b_ada.reshape(DEPTH, 1, n))


def _norm_kernel(x_ref, g_ref, sc_ref, sh_ref, o_ref):
    x = x_ref[...]
    r = lax.rsqrt(jnp.mean(x * x, axis=-1, keepdims=True) + EPS)
    o_ref[...] = ((x * r * g_ref[0]) * (1.0 + sc_ref[0]) + sh_ref[0]).astype(o_ref.dtype)


def _mod_norm(x, gain, layer, scale, shift):
    tm = 512
    per_batch = SEQ // tm
    row = lambda i: (i // per_batch, 0, 0)
    return pl.pallas_call(
        _norm_kernel,
        grid=(ROWS // tm,),
        in_specs=[
            pl.BlockSpec((tm, D_MODEL), lambda i: (i, 0)),
            pl.BlockSpec((1, 1, D_MODEL), lambda i: (layer, 0, 0)),
            pl.BlockSpec((1, 1, D_MODEL), row),
            pl.BlockSpec((1, 1, D_MODEL), row),
        ],
        out_specs=pl.BlockSpec((tm, D_MODEL), lambda i: (i, 0)),
        out_shape=jax.ShapeDtypeStruct((ROWS, D_MODEL), BF16),
        compiler_params=_params("arbitrary"),
        name="mod_norm",
    )(x, gain.reshape(DEPTH, 1, D_MODEL), scale, shift)


def _final_norm_kernel(x_ref, g_ref, o_ref):
    x = x_ref[...]
    r = lax.rsqrt(jnp.mean(x * x, axis=-1, keepdims=True) + EPS)
    o_ref[...] = x * r * g_ref[...]


def _final_norm(x, gain):
    tm = 512
    return pl.pallas_call(
        _final_norm_kernel,
        grid=(ROWS // tm,),
        in_specs=[
            pl.BlockSpec((tm, D_MODEL), lambda i: (i, 0)),
            pl.BlockSpec((1, D_MODEL), lambda i: (0, 0)),
        ],
        out_specs=pl.BlockSpec((tm, D_MODEL), lambda i: (i, 0)),
        out_shape=jax.ShapeDtypeStruct((ROWS, D_MODEL), F32),
        compiler_params=_params("arbitrary"),
        name="final_norm",
    )(x, gain.reshape(1, D_MODEL))


def _cast_chunk(w_ref, scr, slot, step, col0=0):
    rows, cols = w_ref.shape[1], w_ref.shape[2]
    r0 = pl.multiple_of(step * rows, rows)
    scr[slot, pl.ds(r0, rows), col0:col0 + cols] = w_ref[0].astype(BF16)


def _ws_ids():
    j = pl.program_id(0)
    return j, pl.program_id(1), j % 2


def _w_spec(layer, rows, cols, n_tiles, col_tile0=0, row_chunk0=0):
    return pl.BlockSpec((1, rows, cols),
                        lambda j, i: (layer, row_chunk0 + i, col_tile0 + jnp.minimum(j, n_tiles - 1)))


def _a_spec(tm, k, col_block=0):
    return pl.BlockSpec((tm, k), lambda j, i: (jnp.where(j > 0, i, 0), col_block))


def _o_spec(tm, tn):
    return pl.BlockSpec((tm, tn), lambda j, i: (jnp.where(j > 0, i, 0), jnp.maximum(j - 1, 0)))


def _proj_kernel(a_ref, w_ref, o_ref, scr, *, scale):
    j, i, slot = _ws_ids()

    @pl.when(j == 0)
    def _():
        _cast_chunk(w_ref, scr, slot, i)

    @pl.when(j > 0)
    def _():
        acc = _dot(a_ref[...], scr[1 - slot])
        if scale != 1.0:
            acc = acc * scale
        o_ref[...] = acc.astype(o_ref.dtype)
        _cast_chunk(w_ref, scr, slot, i)


def _proj(a, w, layer, col0, n, out_dtype, scale=1.0, tm=1024, tn=512):
    m, k = a.shape
    ni, nj = m // tm, n // tn
    return pl.pallas_call(
        functools.partial(_proj_kernel, scale=scale),
        grid=(nj + 1, ni),
        in_specs=[_a_spec(tm, k), _w_spec(layer, k // ni, tn, nj, col_tile0=col0 // tn)],
        out_specs=_o_spec(tm, tn),
        out_shape=jax.ShapeDtypeStruct((m, n), out_dtype),
        scratch_shapes=[pltpu.VMEM((2, k, tn), BF16)],
        compiler_params=_params("arbitrary", "arbitrary"),
        name="proj",
    )(a, w)


POOL_PAD = 16


def _pool_kernel(u_ref, w_ref, s_ref, o_ref, a_ref, b_ref):
    g = pl.program_id(1)
    zeros = jnp.zeros((POOL_PAD, POOL_GROUP), F32)
    a_ref[0:POOL_PAD, :] = zeros
    b_ref[0:POOL_PAD, :] = zeros
    body = pl.ds(POOL_PAD, SEQ)
    a_ref[body, :] = u_ref[...]
    row = lax.broadcasted_iota(jnp.int32, (SEQ, 1), 0)

    def lagged(ref, lag):
        return ref[pl.ds(POOL_PAD - lag, SEQ), :]

    for group in range(N_GROUPS):
        @pl.when(g == group)
        def _():
            src, dst = a_ref, b_ref
            for level in range(group):
                dst[body, :] = src[body, :] + lagged(src, 1 << level)
                src, dst = dst, src
            total = src[body, :] + lagged(src, 1 << group)
            inv_count = 1.0 / jnp.minimum(row + 1, POOL_WINDOWS[group]).astype(F32)
            pooled = total * inv_count - u_ref[...]
            y = _dot(pooled.astype(BF16), w_ref[0, 0].astype(BF16))
            o_ref[...] = (y * s_ref[0]).astype(o_ref.dtype)


def _pool_branch(u, w_mix, scale, layer):
    return pl.pallas_call(
        _pool_kernel,
        grid=(BATCH, N_GROUPS),
        in_specs=[
            pl.BlockSpec((SEQ, POOL_GROUP), lambda b, g: (b, g)),
            pl.BlockSpec((1, 1, POOL_GROUP, POOL_GROUP), lambda b, g: (layer, g, 0, 0)),
            pl.BlockSpec((1, 1, POOL_GROUP), lambda b, g: (layer, 0, g)),
        ],
        out_specs=pl.BlockSpec((SEQ, POOL_GROUP), lambda b, g: (b, g)),
        out_shape=jax.ShapeDtypeStruct((ROWS, POOL_WIDTH), BF16),
        scratch_shapes=[pltpu.VMEM((POOL_PAD + SEQ, POOL_GROUP), F32),
                        pltpu.VMEM((POOL_PAD + SEQ, POOL_GROUP), F32)],
        compiler_params=_params("arbitrary", "arbitrary"),
        name="pool_branch",
    )(u, w_mix, scale.reshape(DEPTH, 1, POOL_WIDTH))


BAND = 2 * BLOCK
PAIRS = Q_PER_KV // 2


def _bias_kernel(rel_ref, bucket_ref, o_ref):
    h = pl.program_id(1)
    bucket = bucket_ref[0]
    for p in range(PAIRS):
        for e in range(2):
            head = h * Q_PER_KV + 2 * p + e
            acc = jnp.full(bucket.shape, NEG, F32)
            for b in range(N_BUCKETS):
                acc = jnp.where(bucket == b, rel_ref[b, head], acc)
            r0 = ((p // 2) * 2 + e) * BAND
            c0 = (p % 2) * BLOCK
            o_ref[0, 0, r0:r0 + BAND, c0:c0 + BLOCK] = acc


def _band_bias(rel_bias, bucket_pair):
    return pl.pallas_call(
        _bias_kernel,
        grid=(2, N_KV_HEADS),
        in_specs=[
            pl.BlockSpec(memory_space=pltpu.SMEM),
            pl.BlockSpec((1, BAND, BLOCK), lambda f, h: (f, 0, 0)),
        ],
        out_specs=pl.BlockSpec((1, 1, PAIRS * BAND, 2 * BLOCK), lambda f, h: (f, h, 0, 0)),
        out_shape=jax.ShapeDtypeStruct((2, N_KV_HEADS, PAIRS * BAND, 2 * BLOCK), F32),
        compiler_params=_params("arbitrary", "arbitrary"),
        name="band_bias",
    )(rel_bias, bucket_pair)


def _bucket_tables():
    i = jnp.arange(BLOCK)[None, :]
    j = jnp.arange(BAND)[:, None]
    dist = i + BLOCK - j
    max_exact = N_BUCKETS // 2
    d = jnp.maximum(dist, 0)
    log_ratio = jnp.log(jnp.maximum(d, 1).astype(F32) / max_exact) / math.log(MAX_DISTANCE / max_exact)
    large = jnp.minimum(max_exact + (log_ratio * (N_BUCKETS - max_exact)).astype(jnp.int32), N_BUCKETS - 1)
    bucket = jnp.where(d < max_exact, d, large)
    in_window = (dist >= 0) & (dist < WINDOW)
    later = jnp.where(in_window, bucket, -1)
    first = jnp.where(j >= BLOCK, later, -1)
    return jnp.stack([first, later]).astype(jnp.int32)


def _attn_kernel(sink_ref, q_ref, kvp_ref, kvc_ref, bias_ref, o_ref, *, layer):
    lane = lax.broadcasted_iota(jnp.int32, (1, 2 * HEAD_DIM), 1)
    low = lane < HEAD_DIM
    left = lax.broadcasted_iota(jnp.int32, (1, 2 * BLOCK), 1) < BLOCK
    ones = jnp.ones((16, BAND), BF16)
    nt = (((1,), (1,)), ((), ()))
    keys, values = [], []
    for slab in range(N_KV_HEADS // 2):
        c = slab * 2 * HEAD_DIM
        k_slab = jnp.concatenate([kvp_ref[:, c:c + 2 * HEAD_DIM], kvc_ref[:, c:c + 2 * HEAD_DIM]], axis=0)
        v_slab = jnp.concatenate([kvp_ref[:, KV_WIDTH + c:KV_WIDTH + c + 2 * HEAD_DIM],
                                  kvc_ref[:, KV_WIDTH + c:KV_WIDTH + c + 2 * HEAD_DIM]], axis=0)
        k_swap = pltpu.roll(k_slab, HEAD_DIM, axis=1)
        v_t = v_slab.astype(F32).T.astype(BF16)
        zero = jnp.zeros_like(k_slab)
        for hh in range(2):
            k_on_low, k_on_high = (k_slab, k_swap) if hh == 0 else (k_swap, k_slab)
            keys.append((jnp.where(low, k_on_low, zero), jnp.where(low, zero, k_on_high)))
            values.append(jnp.concatenate([v_t[hh * HEAD_DIM:(hh + 1) * HEAD_DIM], ones], axis=0))

    def scores(h, t):
        c0 = h * Q_PER_KV * HEAD_DIM + t * 2 * BLOCK
        q2 = jnp.concatenate([q_ref[:, c0:c0 + BLOCK], q_ref[:, c0 + BLOCK:c0 + 2 * BLOCK]], axis=0)
        return [lax.dot_general(keys[h][e], q2, nt, preferred_element_type=F32)
                + bias_ref[0, h, (t * 2 + e) * BAND:(t * 2 + e + 1) * BAND, :] for e in range(2)]

    def softmax(h, t, s_pair):
        probs, tails = [], []
        for e in range(2):
            head0 = h * Q_PER_KV + 4 * t + e
            sink = jnp.where(left, sink_ref[layer, head0], sink_ref[layer, head0 + 2])
            m = jnp.maximum(jnp.max(s_pair[e], axis=0, keepdims=True), sink)
            probs.append(jnp.exp(s_pair[e] - m).astype(BF16))
            tails.append(jnp.exp(sink - m))
        return probs, tails

    def finish(h, t, probs, tails):
        for pp in range(2):
            cols = slice(pp * BLOCK, (pp + 1) * BLOCK)
            p_t = jnp.concatenate([probs[0][:, cols], probs[1][:, cols]], axis=1)
            tail = jnp.concatenate([tails[0][:, cols], tails[1][:, cols]], axis=1)
            res = _dot(values[h], p_t)
            out_t = res[:HEAD_DIM] * (1.0 / (res[HEAD_DIM:HEAD_DIM + 1] + tail))
            pair_t = jnp.concatenate([out_t[:, :BLOCK], out_t[:, BLOCK:]], axis=0)
            o0 = h * Q_PER_KV * HEAD_DIM + (2 * t + pp) * BLOCK
            o_ref[:, o0:o0 + BLOCK] = pair_t.T.astype(o_ref.dtype)

    groups = [(h, t) for h in range(N_KV_HEADS) for t in range(PAIRS // 2)]
    pending = scores(*groups[0])
    ready = None
    for n, group in enumerate(groups):
        current = pending
        if n + 1 < len(groups):
            pending = scores(*groups[n + 1])
        weights = softmax(*group, current)
        if ready is not None:
            finish(*groups[n - 1], *ready)
        ready = weights
    finish(*groups[-1], *ready)


def _attention(q, kv, sinks, bias, layer):
    n_steps = ROWS // BLOCK
    return pl.pallas_call(
        functools.partial(_attn_kernel, layer=layer),
        grid=(n_steps,),
        in_specs=[
            pl.BlockSpec(memory_space=pltpu.SMEM),
            pl.BlockSpec((BLOCK, Q_WIDTH), lambda r: (r, 0)),
            pl.BlockSpec((BLOCK, 2 * KV_WIDTH), lambda r: (jnp.maximum(r - 1, 0), 0)),
            pl.BlockSpec((BLOCK, 2 * KV_WIDTH), lambda r: (r, 0)),
            pl.BlockSpec((1, N_KV_HEADS, PAIRS * BAND, 2 * BLOCK),
                         lambda r: (jnp.minimum(r % N_BLOCKS, 1), 0, 0, 0)),
        ],
        out_specs=pl.BlockSpec((BLOCK, Q_WIDTH), lambda r: (r, 0)),
        out_shape=jax.ShapeDtypeStruct((ROWS, Q_WIDTH), BF16),
        compiler_params=_params("arbitrary"),
        name="swa_attention",
    )(sinks, q, kv, kv, bias)


def _merge_kernel(h_ref, yp_ref, ya_ref, wgp_ref, wga_ref, wbp_ref, wba_ref, o_ref,
                  sgp, sga, sbp, sba):
    j, i, slot = _ws_ids()

    def cast():
        _cast_chunk(wgp_ref, sgp, slot, i)
        _cast_chunk(wga_ref, sga, slot, i)
        _cast_chunk(wbp_ref, sbp, slot, i)
        _cast_chunk(wba_ref, sba, slot, i)

    @pl.when(j == 0)
    def _():
        cast()

    @pl.when(j > 0)
    def _():
        h = h_ref[...]
        gate_pool = jax.nn.sigmoid(_dot(h, sgp[1 - slot]))
        gate_attn = jax.nn.sigmoid(_dot(h, sga[1 - slot]))
        y_pool = _dot(yp_ref[...], sbp[1 - slot])
        y_attn = _dot(ya_ref[...], sba[1 - slot])
        o_ref[...] = (gate_pool * y_pool + gate_attn * y_attn).astype(o_ref.dtype)
        cast()


def _merge(h, yp, ya, w_gate, w_bp, w_ba, layer, tm=512, tn=512):
    ni, nj = ROWS // tm, D_MODEL // tn
    return pl.pallas_call(
        _merge_kernel,
        grid=(nj + 1, ni),
        in_specs=[
            _a_spec(tm, D_MODEL), _a_spec(tm, POOL_WIDTH), _a_spec(tm, Q_WIDTH),
            _w_spec(layer, D_MODEL // ni, tn, nj),
            _w_spec(layer, D_MODEL // ni, tn, nj, col_tile0=nj),
            _w_spec(layer, POOL_WIDTH // ni, tn, nj),
            _w_spec(layer, Q_WIDTH // ni, tn, nj),
        ],
        out_specs=_o_spec(tm, tn),
        out_shape=jax.ShapeDtypeStruct((ROWS, D_MODEL), BF16),
        scratch_shapes=[pltpu.VMEM((2, D_MODEL, tn), BF16), pltpu.VMEM((2, D_MODEL, tn), BF16),
                        pltpu.VMEM((2, POOL_WIDTH, tn), BF16), pltpu.VMEM((2, Q_WIDTH, tn), BF16)],
        compiler_params=_params("arbitrary", "arbitrary"),
        name="gated_merge",
    )(h, yp, ya, w_gate, w_gate, w_bp, w_ba)


def _resid_kernel(a_ref, w_ref, x_ref, g_ref, o_ref, scr):
    j, i, slot = _ws_ids()

    @pl.when(j == 0)
    def _():
        _cast_chunk(w_ref, scr, slot, i)

    @pl.when(j > 0)
    def _():
        o_ref[...] = x_ref[...] + g_ref[0] * _dot(a_ref[...], scr[1 - slot])
        _cast_chunk(w_ref, scr, slot, i)


def _resid_proj(a, w, x, gate, layer, tm, tn):
    m, k = a.shape
    n = x.shape[1]
    ni, nj = m // tm, n // tn
    per_batch = SEQ // tm
    return pl.pallas_call(
        _resid_kernel,
        grid=(nj + 1, ni),
        in_specs=[
            _a_spec(tm, k),
            _w_spec(layer, k // ni, tn, nj),
            _o_spec(tm, tn),
            pl.BlockSpec((1, 1, tn), lambda j, i: (i // per_batch, 0, jnp.maximum(j - 1, 0))),
        ],
        out_specs=_o_spec(tm, tn),
        out_shape=jax.ShapeDtypeStruct((m, n), F32),
        scratch_shapes=[pltpu.VMEM((2, k, tn), BF16)],
        compiler_params=_params("arbitrary", "arbitrary"),
        name="resid_proj",
    )(a, w, x, gate)


FF_TILE = V7X_MXU_DIM


def _swiglu_kernel(a_ref, wa_ref, wb_ref, o_ref, scr):
    j, i, slot = _ws_ids()

    def cast():
        _cast_chunk(wa_ref, scr, slot, i)
        _cast_chunk(wb_ref, scr, slot, i, col0=FF_TILE)

    @pl.when(j == 0)
    def _():
        cast()

    @pl.when(j > 0)
    def _():
        ab = _dot(a_ref[...], scr[1 - slot])
        a = ab[:, :FF_TILE]
        b = ab[:, FF_TILE:]
        o_ref[...] = ((a * jax.nn.sigmoid(a)) * b).astype(o_ref.dtype)
        cast()


def _swiglu(h2, w_ffn_in, layer, tm=2048):
    ni, nj = ROWS // tm, D_FF // FF_TILE
    return pl.pallas_call(
        _swiglu_kernel,
        grid=(nj + 1, ni),
        in_specs=[
            _a_spec(tm, D_MODEL),
            _w_spec(layer, D_MODEL // ni, FF_TILE, nj),
            _w_spec(layer, D_MODEL // ni, FF_TILE, nj, col_tile0=nj),
        ],
        out_specs=_o_spec(tm, FF_TILE),
        out_shape=jax.ShapeDtypeStruct((ROWS, D_FF), BF16),
        scratch_shapes=[pltpu.VMEM((2, D_MODEL, 2 * FF_TILE), BF16)],
        compiler_params=_params("arbitrary", "arbitrary"),
        name="swiglu_up",
    )(h2, w_ffn_in, w_ffn_in)


def kernel(x, c, w_ada, b_ada, norm1, w_in, w_pool_mix, pool_scale, sinks, rel_bias,
           w_branch_pool, w_branch_attn, w_gate, w_out, norm2, w_ffn_in, w_ffn_out,
           final_norm):
    c_pad = jnp.pad(c, ((0, 8 - BATCH), (0, 0)))
    mod = _ada(c_pad, w_ada, b_ada)[:, :BATCH]
    bias = _band_bias(rel_bias, _bucket_tables())
    xr = x.reshape(ROWS, D_MODEL)

    for l in range(DEPTH):
        sh1, sc1, g1, sh2, sc2, g2 = [m.reshape(BATCH, 1, D_MODEL)
                                      for m in jnp.split(mod[l], N_MOD, axis=-1)]
        h = _mod_norm(xr, norm1, l, sc1, sh1)
        u = _proj(h, w_in, l, 0, POOL_WIDTH, F32, tm=512, tn=1024)
        q = _proj(h, w_in, l, POOL_WIDTH, Q_WIDTH, BF16, scale=HEAD_DIM ** -0.5, tm=512, tn=1024)
        kv = _proj(h, w_in, l, POOL_WIDTH + Q_WIDTH, 2 * KV_WIDTH, BF16)
        yp = _pool_branch(u, w_pool_mix, pool_scale, l)
        ya = _attention(q, kv, sinks, bias, l)
        merged = _merge(h, yp, ya, w_gate, w_branch_pool, w_branch_attn, l)
        xr = _resid_proj(merged, w_out, xr, g1, l, tm=512, tn=1024)

        h2 = _mod_norm(xr, norm2, l, sc2, sh2)
        act = _swiglu(h2, w_ffn_in, l)
        xr = _resid_proj(act, w_ffn_out, xr, g2, l, tm=512, tn=512)

    return _final_norm(xr, final_norm).reshape(BATCH, SEQ, D_MODEL)
```

```python
import functools
import math

import jax
import jax.numpy as jnp
from jax import lax
from jax.experimental import pallas as pl
from jax.experimental.pallas import tpu as pltpu

D_MODEL = 4096
BATCH = 4
SEQ = 2048
ROWS = BATCH * SEQ
DEPTH = 2
POOL_WINDOWS = (2, 4, 8, 16)
N_GROUPS = len(POOL_WINDOWS)
POOL_WIDTH = D_MODEL // 2
POOL_GROUP = POOL_WIDTH // N_GROUPS
HEAD_DIM = 64
N_Q_HEADS = (D_MODEL // 2) // HEAD_DIM
N_KV_HEADS = N_Q_HEADS // 8
Q_PER_KV = N_Q_HEADS // N_KV_HEADS
Q_WIDTH = N_Q_HEADS * HEAD_DIM
KV_WIDTH = N_KV_HEADS * HEAD_DIM
WINDOW = 128
BLOCK = 128
N_BLOCKS = SEQ // BLOCK
N_BUCKETS = 32
MAX_DISTANCE = 128
IN_WIDTH = POOL_WIDTH + Q_WIDTH + 2 * KV_WIDTH
D_FF = -(-(8 * D_MODEL) // (3 * 256)) * 256
N_MOD = 6
EPS = 1e-6

V7X_VMEM_BYTES = 64 * 1024 * 1024
VMEM_LIMIT = V7X_VMEM_BYTES - 8 * 1024 * 1024
V7X_MXU_DIM = 256

NEG = -1e30

BF16 = jnp.bfloat16
F32 = jnp.float32


def _params(*sem):
    return pltpu.CompilerParams(dimension_semantics=sem, vmem_limit_bytes=VMEM_LIMIT)


def _dot(a, b):
    return jnp.dot(a, b, preferred_element_type=F32)


ADA_ROWS = 128


def _ada_kernel(c_ref, w_ref, b_ref, o_ref):
    k = pl.program_id(1)

    @pl.when(k == 0)
    def _():
        o_ref[0] = jnp.broadcast_to(b_ref[0], o_ref.shape[1:])

    c = c_ref[...]
    s = (c * jax.nn.sigmoid(c)).astype(BF16)
    o_ref[0] += _dot(s, w_ref[0].astype(BF16))


def _ada(c_pad, w_ada, b_ada):
    n = N_MOD * D_MODEL
    return pl.pallas_call(
        _ada_kernel,
        grid=(DEPTH, D_MODEL // ADA_ROWS),
        in_specs=[
            pl.BlockSpec((8, ADA_ROWS), lambda l, k: (0, k)),
            pl.BlockSpec((1, ADA_ROWS, n), lambda l, k: (l, k, 0)),
            pl.BlockSpec((1, 1, n), lambda l, k: (l, 0, 0)),
        ],
        out_specs=pl.BlockSpec((1, 8, n), lambda l, k: (l, 0, 0)),
        out_shape=jax.ShapeDtypeStruct((DEPTH, 8, n), F32),
        compiler_params=_params("arbitrary", "arbitrary"),
        name="ada_mod",
    )(c_pad, w_ada, b_ada.reshape(DEPTH, 1, n))


NORM_CHUNK = 64


def _norm_kernel(x_ref, g_ref, sc_ref, sh_ref, o_ref):
    gain = g_ref[0] * (1.0 + sc_ref[0])
    shift = sh_ref[0]

    def chunk(c, carry):
        rows = pl.ds(pl.multiple_of(c * NORM_CHUNK, NORM_CHUNK), NORM_CHUNK)
        x = x_ref[rows, :]
        r = lax.rsqrt(jnp.mean(x * x, axis=-1, keepdims=True) + EPS)
        o_ref[rows, :] = ((x * r) * gain + shift).astype(o_ref.dtype)
        return carry

    lax.fori_loop(0, x_ref.shape[0] // NORM_CHUNK, chunk, 0)


def _mod_norm(x, gain, layer, scale, shift):
    tm = 512
    per_batch = SEQ // tm
    row = lambda i: (i // per_batch, 0, 0)
    return pl.pallas_call(
        _norm_kernel,
        grid=(ROWS // tm,),
        in_specs=[
            pl.BlockSpec((tm, D_MODEL), lambda i: (i, 0)),
            pl.BlockSpec((1, 1, D_MODEL), lambda i: (layer, 0, 0)),
            pl.BlockSpec((1, 1, D_MODEL), row),
            pl.BlockSpec((1, 1, D_MODEL), row),
        ],
        out_specs=pl.BlockSpec((tm, D_MODEL), lambda i: (i, 0)),
        out_shape=jax.ShapeDtypeStruct((ROWS, D_MODEL), BF16),
        compiler_params=_params("arbitrary"),
        name="mod_norm",
    )(x, gain.reshape(DEPTH, 1, D_MODEL), scale, shift)


def _final_norm_kernel(x_ref, g_ref, o_ref):
    gain = g_ref[...]

    def chunk(c, carry):
        rows = pl.ds(pl.multiple_of(c * NORM_CHUNK, NORM_CHUNK), NORM_CHUNK)
        x = x_ref[rows, :]
        r = lax.rsqrt(jnp.mean(x * x, axis=-1, keepdims=True) + EPS)
        o_ref[rows, :] = x * r * gain
        return carry

    lax.fori_loop(0, x_ref.shape[0] // NORM_CHUNK, chunk, 0)


def _final_norm(x, gain):
    tm = 512
    return pl.pallas_call(
        _final_norm_kernel,
        grid=(ROWS // tm,),
        in_specs=[
            pl.BlockSpec((tm, D_MODEL), lambda i: (i, 0)),
            pl.BlockSpec((1, D_MODEL), lambda i: (0, 0)),
        ],
        out_specs=pl.BlockSpec((tm, D_MODEL), lambda i: (i, 0)),
        out_shape=jax.ShapeDtypeStruct((ROWS, D_MODEL), F32),
        compiler_params=_params("arbitrary"),
        name="final_norm",
    )(x, gain.reshape(1, D_MODEL))


def _cast_chunk(w_ref, scr, slot, step, col0=0):
    rows, cols = w_ref.shape[1], w_ref.shape[2]
    r0 = pl.multiple_of(step * rows, rows)
    scr[slot, pl.ds(r0, rows), col0:col0 + cols] = w_ref[0].astype(BF16)


def _ws_ids():
    j = pl.program_id(0)
    return j, pl.program_id(1), j % 2


def _w_spec(layer, rows, cols, n_tiles, col_tile0=0, row_chunk0=0):
    return pl.BlockSpec((1, rows, cols),
                        lambda j, i: (layer, row_chunk0 + i, col_tile0 + jnp.minimum(j, n_tiles - 1)))


def _a_spec(tm, k, col_block=0):
    return pl.BlockSpec((tm, k), lambda j, i: (jnp.where(j > 0, i, 0), col_block))


def _o_spec(tm, tn):
    return pl.BlockSpec((tm, tn), lambda j, i: (jnp.where(j > 0, i, 0), jnp.maximum(j - 1, 0)))


PROJ_TN = 512
PROJ_U_TILES = POOL_WIDTH // PROJ_TN
PROJ_Q_TILES = Q_WIDTH // PROJ_TN
PROJ_KV_TILES = 2 * KV_WIDTH // PROJ_TN


def _proj_kernel(a_ref, w_ref, u_ref, q_ref, kv_ref, scr):
    j, i, slot = _ws_ids()
    q_start = 1 + PROJ_U_TILES
    kv_start = q_start + PROJ_Q_TILES

    @pl.when(j == 0)
    def _():
        _cast_chunk(w_ref, scr, slot, i)

    @pl.when((j >= 1) & (j < q_start))
    def _():
        u_ref[...] = _dot(a_ref[...], scr[1 - slot])
        _cast_chunk(w_ref, scr, slot, i)

    @pl.when((j >= q_start) & (j < kv_start))
    def _():
        q_ref[...] = (_dot(a_ref[...], scr[1 - slot]) * HEAD_DIM ** -0.5).astype(q_ref.dtype)
        _cast_chunk(w_ref, scr, slot, i)

    @pl.when(j >= kv_start)
    def _():
        kv_ref[...] = _dot(a_ref[...], scr[1 - slot]).astype(kv_ref.dtype)
        _cast_chunk(w_ref, scr, slot, i)


def _proj_out_spec(tm, ni, first, n_tiles):
    def index(j, i):
        row = jnp.where(j < first, 0, jnp.where(j < first + n_tiles, i, ni - 1))
        return row, jnp.clip(j - first, 0, n_tiles - 1)
    return pl.BlockSpec((tm, PROJ_TN), index)


def _proj(a, w, layer, tm=1024):
    m, k = a.shape
    ni, nj = m // tm, IN_WIDTH // PROJ_TN
    q_start = 1 + PROJ_U_TILES
    return pl.pallas_call(
        _proj_kernel,
        grid=(nj + 1, ni),
        in_specs=[_a_spec(tm, k), _w_spec(layer, k // ni, PROJ_TN, nj)],
        out_specs=[_proj_out_spec(tm, ni, 1, PROJ_U_TILES),
                   _proj_out_spec(tm, ni, q_start, PROJ_Q_TILES),
                   _proj_out_spec(tm, ni, q_start + PROJ_Q_TILES, PROJ_KV_TILES)],
        out_shape=[jax.ShapeDtypeStruct((m, POOL_WIDTH), F32),
                   jax.ShapeDtypeStruct((m, Q_WIDTH), BF16),
                   jax.ShapeDtypeStruct((m, 2 * KV_WIDTH), BF16)],
        scratch_shapes=[pltpu.VMEM((2, k, PROJ_TN), BF16)],
        compiler_params=_params("arbitrary", "arbitrary"),
        name="proj",
    )(a, w)


POOL_PAD = 16


def _pool_kernel(u_ref, w_ref, s_ref, o_ref, a_ref, b_ref):
    g = pl.program_id(1)
    zeros = jnp.zeros((POOL_PAD, POOL_GROUP), F32)
    a_ref[0:POOL_PAD, :] = zeros
    b_ref[0:POOL_PAD, :] = zeros
    body = pl.ds(POOL_PAD, SEQ)
    a_ref[body, :] = u_ref[...]
    row = lax.broadcasted_iota(jnp.int32, (SEQ, 1), 0)

    def lagged(ref, lag):
        return ref[pl.ds(POOL_PAD - lag, SEQ), :]

    for group in range(N_GROUPS):
        @pl.when(g == group)
        def _():
            src, dst = a_ref, b_ref
            for level in range(group):
                dst[body, :] = src[body, :] + lagged(src, 1 << level)
                src, dst = dst, src
            total = src[body, :] + lagged(src, 1 << group)
            inv_count = 1.0 / jnp.minimum(row + 1, POOL_WINDOWS[group]).astype(F32)
            pooled = total * inv_count - u_ref[...]
            y = _dot(pooled.astype(BF16), w_ref[0, 0].astype(BF16))
            o_ref[...] = (y * s_ref[0]).astype(o_ref.dtype)


def _pool_branch(u, w_mix, scale, layer):
    return pl.pallas_call(
        _pool_kernel,
        grid=(BATCH, N_GROUPS),
        in_specs=[
            pl.BlockSpec((SEQ, POOL_GROUP), lambda b, g: (b, g)),
            pl.BlockSpec((1, 1, POOL_GROUP, POOL_GROUP), lambda b, g: (layer, g, 0, 0)),
            pl.BlockSpec((1, 1, POOL_GROUP), lambda b, g: (layer, 0, g)),
        ],
        out_specs=pl.BlockSpec((SEQ, POOL_GROUP), lambda b, g: (b, g)),
        out_shape=jax.ShapeDtypeStruct((ROWS, POOL_WIDTH), BF16),
        scratch_shapes=[pltpu.VMEM((POOL_PAD + SEQ, POOL_GROUP), F32),
                        pltpu.VMEM((POOL_PAD + SEQ, POOL_GROUP), F32)],
        compiler_params=_params("arbitrary", "arbitrary"),
        name="pool_branch",
    )(u, w_mix, scale.reshape(DEPTH, 1, POOL_WIDTH))


BAND = 2 * BLOCK
PAIRS = Q_PER_KV // 2


def _bias_kernel(rel_ref, bucket_ref, o_ref):
    h = pl.program_id(1)
    bucket = bucket_ref[0]
    for p in range(PAIRS):
        for e in range(2):
            head = h * Q_PER_KV + 2 * p + e
            acc = jnp.full(bucket.shape, NEG, F32)
            for b in range(N_BUCKETS):
                acc = jnp.where(bucket == b, rel_ref[b, head], acc)
            r0 = ((p // 2) * 2 + e) * BAND
            c0 = (p % 2) * BLOCK
            o_ref[0, 0, r0:r0 + BAND, c0:c0 + BLOCK] = acc


def _band_bias(rel_bias, bucket_pair):
    return pl.pallas_call(
        _bias_kernel,
        grid=(2, N_KV_HEADS),
        in_specs=[
            pl.BlockSpec(memory_space=pltpu.SMEM),
            pl.BlockSpec((1, BAND, BLOCK), lambda f, h: (f, 0, 0)),
        ],
        out_specs=pl.BlockSpec((1, 1, PAIRS * BAND, 2 * BLOCK), lambda f, h: (f, h, 0, 0)),
        out_shape=jax.ShapeDtypeStruct((2, N_KV_HEADS, PAIRS * BAND, 2 * BLOCK), F32),
        compiler_params=_params("arbitrary", "arbitrary"),
        name="band_bias",
    )(rel_bias, bucket_pair)


def _bucket_tables():
    i = jnp.arange(BLOCK)[None, :]
    j = jnp.arange(BAND)[:, None]
    dist = i + BLOCK - j
    max_exact = N_BUCKETS // 2
    d = jnp.maximum(dist, 0)
    log_ratio = jnp.log(jnp.maximum(d, 1).astype(F32) / max_exact) / math.log(MAX_DISTANCE / max_exact)
    large = jnp.minimum(max_exact + (log_ratio * (N_BUCKETS - max_exact)).astype(jnp.int32), N_BUCKETS - 1)
    bucket = jnp.where(d < max_exact, d, large)
    in_window = (dist >= 0) & (dist < WINDOW)
    later = jnp.where(in_window, bucket, -1)
    first = jnp.where(j >= BLOCK, later, -1)
    return jnp.stack([first, later]).astype(jnp.int32)


def _attn_kernel(sink_ref, q_ref, kvp_ref, kvc_ref, bias_ref, o_ref, *, layer):
    lane = lax.broadcasted_iota(jnp.int32, (1, 2 * HEAD_DIM), 1)
    low = lane < HEAD_DIM
    left = lax.broadcasted_iota(jnp.int32, (1, 2 * BLOCK), 1) < BLOCK
    ones = jnp.ones((16, BAND), BF16)
    nt = (((1,), (1,)), ((), ()))
    keys, values = [], []
    for slab in range(N_KV_HEADS // 2):
        c = slab * 2 * HEAD_DIM
        k_slab = jnp.concatenate([kvp_ref[:, c:c + 2 * HEAD_DIM], kvc_ref[:, c:c + 2 * HEAD_DIM]], axis=0)
        v_slab = jnp.concatenate([kvp_ref[:, KV_WIDTH + c:KV_WIDTH + c + 2 * HEAD_DIM],
                                  kvc_ref[:, KV_WIDTH + c:KV_WIDTH + c + 2 * HEAD_DIM]], axis=0)
        k_swap = pltpu.roll(k_slab, HEAD_DIM, axis=1)
        v_t = v_slab.astype(F32).T.astype(BF16)
        zero = jnp.zeros_like(k_slab)
        for hh in range(2):
            k_on_low, k_on_high = (k_slab, k_swap) if hh == 0 else (k_swap, k_slab)
            keys.append((jnp.where(low, k_on_low, zero), jnp.where(low, zero, k_on_high)))
            values.append(jnp.concatenate([v_t[hh * HEAD_DIM:(hh + 1) * HEAD_DIM], ones], axis=0))

    def scores(h, t):
        c0 = h * Q_PER_KV * HEAD_DIM + t * 2 * BLOCK
        q2 = jnp.concatenate([q_ref[:, c0:c0 + BLOCK], q_ref[:, c0 + BLOCK:c0 + 2 * BLOCK]], axis=0)
        return [lax.dot_general(keys[h][e], q2, nt, preferred_element_type=F32)
                + bias_ref[0, h, (t * 2 + e) * BAND:(t * 2 + e + 1) * BAND, :] for e in range(2)]

    def softmax(h, t, s_pair):
        probs, tails = [], []
        for e in range(2):
            head0 = h * Q_PER_KV + 4 * t + e
            sink = jnp.where(left, sink_ref[layer, head0], sink_ref[layer, head0 + 2])
            m = jnp.maximum(jnp.max(s_pair[e], axis=0, keepdims=True), sink)
            probs.append(jnp.exp(s_pair[e] - m).astype(BF16))
            tails.append(jnp.exp(sink - m))
        return probs, tails

    def finish(h, t, probs, tails):
        for pp in range(2):
            cols = slice(pp * BLOCK, (pp + 1) * BLOCK)
            p_t = jnp.concatenate([probs[0][:, cols], probs[1][:, cols]], axis=1)
            tail = jnp.concatenate([tails[0][:, cols], tails[1][:, cols]], axis=1)
            res = _dot(values[h], p_t)
            out_t = res[:HEAD_DIM] * (1.0 / (res[HEAD_DIM:HEAD_DIM + 1] + tail))
            pair_t = jnp.concatenate([out_t[:, :BLOCK], out_t[:, BLOCK:]], axis=0)
            o0 = h * Q_PER_KV * HEAD_DIM + (2 * t + pp) * BLOCK
            o_ref[:, o0:o0 + BLOCK] = pair_t.T.astype(o_ref.dtype)

    groups = [(h, t) for h in range(N_KV_HEADS) for t in range(PAIRS // 2)]
    pending = scores(*groups[0])
    ready = None
    for n, group in enumerate(groups):
        current = pending
        if n + 1 < len(groups):
            pending = scores(*groups[n + 1])
        weights = softmax(*group, current)
        if ready is not None:
            finish(*groups[n - 1], *ready)
        ready = weights
    finish(*groups[-1], *ready)


def _attention(q, kv, sinks, bias, layer):
    n_steps = ROWS // BLOCK
    return pl.pallas_call(
        functools.partial(_attn_kernel, layer=layer),
        grid=(n_steps,),
        in_specs=[
            pl.BlockSpec(memory_space=pltpu.SMEM),
            pl.BlockSpec((BLOCK, Q_WIDTH), lambda r: (r, 0)),
            pl.BlockSpec((BLOCK, 2 * KV_WIDTH), lambda r: (jnp.maximum(r - 1, 0), 0)),
            pl.BlockSpec((BLOCK, 2 * KV_WIDTH), lambda r: (r, 0)),
            pl.BlockSpec((1, N_KV_HEADS, PAIRS * BAND, 2 * BLOCK),
                         lambda r: (jnp.minimum(r % N_BLOCKS, 1), 0, 0, 0)),
        ],
        out_specs=pl.BlockSpec((BLOCK, Q_WIDTH), lambda r: (r, 0)),
        out_shape=jax.ShapeDtypeStruct((ROWS, Q_WIDTH), BF16),
        compiler_params=_params("arbitrary"),
        name="swa_attention",
    )(sinks, q, kv, kv, bias)


def _merge_kernel(h_ref, yp_ref, ya_ref, wgp_ref, wga_ref, wbp_ref, wba_ref, o_ref,
                  sgp, sga, sbp, sba):
    j, i, slot = _ws_ids()

    def cast():
        _cast_chunk(wgp_ref, sgp, slot, i)
        _cast_chunk(wga_ref, sga, slot, i)
        _cast_chunk(wbp_ref, sbp, slot, i)
        _cast_chunk(wba_ref, sba, slot, i)

    @pl.when(j == 0)
    def _():
        cast()

    @pl.when(j > 0)
    def _():
        h = h_ref[...]
        gate_pool = jax.nn.sigmoid(_dot(h, sgp[1 - slot]))
        gate_attn = jax.nn.sigmoid(_dot(h, sga[1 - slot]))
        y_pool = _dot(yp_ref[...], sbp[1 - slot])
        y_attn = _dot(ya_ref[...], sba[1 - slot])
        o_ref[...] = (gate_pool * y_pool + gate_attn * y_attn).astype(o_ref.dtype)
        cast()


def _merge(h, yp, ya, w_gate, w_bp, w_ba, layer, tm=512, tn=512):
    ni, nj = ROWS // tm, D_MODEL // tn
    return pl.pallas_call(
        _merge_kernel,
        grid=(nj + 1, ni),
        in_specs=[
            _a_spec(tm, D_MODEL), _a_spec(tm, POOL_WIDTH), _a_spec(tm, Q_WIDTH),
            _w_spec(layer, D_MODEL // ni, tn, nj),
            _w_spec(layer, D_MODEL // ni, tn, nj, col_tile0=nj),
            _w_spec(layer, POOL_WIDTH // ni, tn, nj),
            _w_spec(layer, Q_WIDTH // ni, tn, nj),
        ],
        out_specs=_o_spec(tm, tn),
        out_shape=jax.ShapeDtypeStruct((ROWS, D_MODEL), BF16),
        scratch_shapes=[pltpu.VMEM((2, D_MODEL, tn), BF16), pltpu.VMEM((2, D_MODEL, tn), BF16),
                        pltpu.VMEM((2, POOL_WIDTH, tn), BF16), pltpu.VMEM((2, Q_WIDTH, tn), BF16)],
        compiler_params=_params("arbitrary", "arbitrary"),
        name="gated_merge",
    )(h, yp, ya, w_gate, w_gate, w_bp, w_ba)


def _resid_kernel(a_ref, w_ref, x_ref, g_ref, o_ref, scr):
    j, i, slot = _ws_ids()

    @pl.when(j == 0)
    def _():
        _cast_chunk(w_ref, scr, slot, i)

    @pl.when(j > 0)
    def _():
        o_ref[...] = x_ref[...] + g_ref[0] * _dot(a_ref[...], scr[1 - slot])
        _cast_chunk(w_ref, scr, slot, i)


def _resid_proj(a, w, x, gate, layer, tm, tn):
    m, k = a.shape
    n = x.shape[1]
    ni, nj = m // tm, n // tn
    per_batch = SEQ // tm
    return pl.pallas_call(
        _resid_kernel,
        grid=(nj + 1, ni),
        in_specs=[
            _a_spec(tm, k),
            _w_spec(layer, k // ni, tn, nj),
            _o_spec(tm, tn),
            pl.BlockSpec((1, 1, tn), lambda j, i: (i // per_batch, 0, jnp.maximum(j - 1, 0))),
        ],
        out_specs=_o_spec(tm, tn),
        out_shape=jax.ShapeDtypeStruct((m, n), F32),
        scratch_shapes=[pltpu.VMEM((2, k, tn), BF16)],
        compiler_params=_params("arbitrary", "arbitrary"),
        name="resid_proj",
    )(a, w, x, gate)


FF_TILE = V7X_MXU_DIM


def _swiglu_kernel(a_ref, wa_ref, wb_ref, o_ref, scr):
    j, i, slot = _ws_ids()

    def cast():
        _cast_chunk(wa_ref, scr, slot, i)
        _cast_chunk(wb_ref, scr, slot, i, col0=FF_TILE)

    @pl.when(j == 0)
    def _():
        cast()

    @pl.when(j > 0)
    def _():
        ab = _dot(a_ref[...], scr[1 - slot])
        a = ab[:, :FF_TILE]
        b = ab[:, FF_TILE:]
        o_ref[...] = ((a * jax.nn.sigmoid(a)) * b).astype(o_ref.dtype)
        cast()


def _swiglu(h2, w_ffn_in, layer, tm=2048):
    ni, nj = ROWS // tm, D_FF // FF_TILE
    return pl.pallas_call(
        _swiglu_kernel,
        grid=(nj + 1, ni),
        in_specs=[
            _a_spec(tm, D_MODEL),
            _w_spec(layer, D_MODEL // ni, FF_TILE, nj),
            _w_spec(layer, D_MODEL // ni, FF_TILE, nj, col_tile0=nj),
        ],
        out_specs=_o_spec(tm, FF_TILE),
        out_shape=jax.ShapeDtypeStruct((ROWS, D_FF), BF16),
        scratch_shapes=[pltpu.VMEM((2, D_MODEL, 2 * FF_TILE), BF16)],
        compiler_params=_params("arbitrary", "arbitrary"),
        name="swiglu_up",
    )(h2, w_ffn_in, w_ffn_in)


def kernel(x, c, w_ada, b_ada, norm1, w_in, w_pool_mix, pool_scale, sinks, rel_bias,
           w_branch_pool, w_branch_attn, w_gate, w_out, norm2, w_ffn_in, w_ffn_out,
           final_norm):
    c_pad = jnp.pad(c, ((0, 8 - BATCH), (0, 0)))
    mod = _ada(c_pad, w_ada, b_ada)[:, :BATCH]
    bias = _band_bias(rel_bias, _bucket_tables())
    xr = x.reshape(ROWS, D_MODEL)

    for l in range(DEPTH):
        sh1, sc1, g1, sh2, sc2, g2 = [m.reshape(BATCH, 1, D_MODEL)
                                      for m in jnp.split(mod[l], N_MOD, axis=-1)]
        h = _mod_norm(xr, norm1, l, sc1, sh1)
        u, q, kv = _proj(h, w_in, l)
        yp = _pool_branch(u, w_pool_mix, pool_scale, l)
        ya = _attention(q, kv, sinks, bias, l)
        merged = _merge(h, yp, ya, w_gate, w_branch_pool, w_branch_attn, l)
        xr = _resid_proj(merged, w_out, xr, g1, l, tm=512, tn=1024)

        h2 = _mod_norm(xr, norm2, l, sc2, sh2)
        act = _swiglu(h2, w_ffn_in, l)
        xr = _resid_proj(act, w_ffn_out, xr, g2, l, tm=512, tn=512)

    return _final_norm(xr, final_norm).reshape(BATCH, SEQ, D_MODEL)
```

```python
import functools
import math

import jax
import jax.numpy as jnp
from jax import lax
from jax.experimental import pallas as pl
from jax.experimental.pallas import tpu as pltpu

D_MODEL = 4096
BATCH = 4
SEQ = 2048
ROWS = BATCH * SEQ
DEPTH = 2
POOL_WINDOWS = (2, 4, 8, 16)
N_GROUPS = len(POOL_WINDOWS)
POOL_WIDTH = D_MODEL // 2
POOL_GROUP = POOL_WIDTH // N_GROUPS
HEAD_DIM = 64
N_Q_HEADS = (D_MODEL // 2) // HEAD_DIM
N_KV_HEADS = N_Q_HEADS // 8
Q_PER_KV = N_Q_HEADS // N_KV_HEADS
Q_WIDTH = N_Q_HEADS * HEAD_DIM
KV_WIDTH = N_KV_HEADS * HEAD_DIM
WINDOW = 128
BLOCK = 128
N_BLOCKS = SEQ // BLOCK
N_BUCKETS = 32
MAX_DISTANCE = 128
IN_WIDTH = POOL_WIDTH + Q_WIDTH + 2 * KV_WIDTH
D_FF = -(-(8 * D_MODEL) // (3 * 256)) * 256
N_MOD = 6
EPS = 1e-6

V7X_VMEM_BYTES = 64 * 1024 * 1024
VMEM_LIMIT = V7X_VMEM_BYTES - 8 * 1024 * 1024
V7X_MXU_DIM = 256

NEG = -1e30

BF16 = jnp.bfloat16
F32 = jnp.float32


def _params(*sem):
    return pltpu.CompilerParams(dimension_semantics=sem, vmem_limit_bytes=VMEM_LIMIT)


def _dot(a, b):
    return jnp.dot(a, b, preferred_element_type=F32)


ADA_ROWS = 128
N_ADA = N_MOD * D_MODEL


def _silu_bf16(c):
    return (c * jax.nn.sigmoid(c)).astype(BF16)


def _ada_kernel(c_ref, w_ref, b_ref, o_ref):
    k = pl.program_id(0)

    @pl.when(k == 0)
    def _():
        o_ref[...] = jnp.broadcast_to(b_ref[0], o_ref.shape)

    o_ref[...] += _dot(_silu_bf16(c_ref[...]), w_ref[0].astype(BF16))


def _ada_first(c_pad, w_ada, b_ada):
    return pl.pallas_call(
        _ada_kernel,
        grid=(D_MODEL // ADA_ROWS,),
        in_specs=[
            pl.BlockSpec((8, ADA_ROWS), lambda k: (0, k)),
            pl.BlockSpec((1, ADA_ROWS, N_ADA), lambda k: (0, k, 0)),
            pl.BlockSpec((1, 1, N_ADA), lambda k: (0, 0, 0)),
        ],
        out_specs=pl.BlockSpec((8, N_ADA), lambda k: (0, 0)),
        out_shape=jax.ShapeDtypeStruct((8, N_ADA), F32),
        compiler_params=_params("arbitrary"),
        name="ada_mod",
    )(c_pad, w_ada, b_ada.reshape(DEPTH, 1, N_ADA))


NORM_CHUNK = 64


def _norm_kernel(x_ref, g_ref, sc_ref, sh_ref, o_ref):
    gain = g_ref[0] * (1.0 + sc_ref[0])
    shift = sh_ref[0]

    def chunk(c, carry):
        rows = pl.ds(pl.multiple_of(c * NORM_CHUNK, NORM_CHUNK), NORM_CHUNK)
        x = x_ref[rows, :]
        r = lax.rsqrt(jnp.mean(x * x, axis=-1, keepdims=True) + EPS)
        o_ref[rows, :] = ((x * r) * gain + shift).astype(o_ref.dtype)
        return carry

    lax.fori_loop(0, x_ref.shape[0] // NORM_CHUNK, chunk, 0)


def _mod_norm(x, gain, layer, scale, shift):
    tm = 512
    per_batch = SEQ // tm
    row = lambda i: (i // per_batch, 0, 0)
    return pl.pallas_call(
        _norm_kernel,
        grid=(ROWS // tm,),
        in_specs=[
            pl.BlockSpec((tm, D_MODEL), lambda i: (i, 0)),
            pl.BlockSpec((1, 1, D_MODEL), lambda i: (layer, 0, 0)),
            pl.BlockSpec((1, 1, D_MODEL), row),
            pl.BlockSpec((1, 1, D_MODEL), row),
        ],
        out_specs=pl.BlockSpec((tm, D_MODEL), lambda i: (i, 0)),
        out_shape=jax.ShapeDtypeStruct((ROWS, D_MODEL), BF16),
        compiler_params=_params("arbitrary"),
        name="mod_norm",
    )(x, gain.reshape(DEPTH, 1, D_MODEL), scale, shift)


def _final_norm_kernel(x_ref, g_ref, o_ref):
    gain = g_ref[...]

    def chunk(c, carry):
        rows = pl.ds(pl.multiple_of(c * NORM_CHUNK, NORM_CHUNK), NORM_CHUNK)
        x = x_ref[rows, :]
        r = lax.rsqrt(jnp.mean(x * x, axis=-1, keepdims=True) + EPS)
        o_ref[rows, :] = x * r * gain
        return carry

    lax.fori_loop(0, x_ref.shape[0] // NORM_CHUNK, chunk, 0)


def _final_norm(x, gain):
    tm = 512
    return pl.pallas_call(
        _final_norm_kernel,
        grid=(ROWS // tm,),
        in_specs=[
            pl.BlockSpec((tm, D_MODEL), lambda i: (i, 0)),
            pl.BlockSpec((1, D_MODEL), lambda i: (0, 0)),
        ],
        out_specs=pl.BlockSpec((tm, D_MODEL), lambda i: (i, 0)),
        out_shape=jax.ShapeDtypeStruct((ROWS, D_MODEL), F32),
        compiler_params=_params("arbitrary"),
        name="final_norm",
    )(x, gain.reshape(1, D_MODEL))


def _cast_chunk(w_ref, scr, slot, step, col0=0):
    rows, cols = w_ref.shape[1], w_ref.shape[2]
    r0 = pl.multiple_of(step * rows, rows)
    scr[slot, pl.ds(r0, rows), col0:col0 + cols] = w_ref[0].astype(BF16)


def _ws_ids():
    j = pl.program_id(0)
    return j, pl.program_id(1), j % 2


def _w_spec(layer, rows, cols, n_tiles, col_tile0=0, row_chunk0=0):
    return pl.BlockSpec((1, rows, cols),
                        lambda j, i: (layer, row_chunk0 + i, col_tile0 + jnp.minimum(j, n_tiles - 1)))


def _a_spec(tm, k, col_block=0):
    return pl.BlockSpec((tm, k), lambda j, i: (jnp.where(j > 0, i, 0), col_block))


def _o_spec(tm, tn):
    return pl.BlockSpec((tm, tn), lambda j, i: (jnp.where(j > 0, i, 0), jnp.maximum(j - 1, 0)))


PROJ_TN = 512
PROJ_U_TILES = POOL_WIDTH // PROJ_TN
PROJ_Q_TILES = Q_WIDTH // PROJ_TN
PROJ_KV_TILES = 2 * KV_WIDTH // PROJ_TN


def _proj_kernel(a_ref, w_ref, u_ref, q_ref, kv_ref, scr):
    j, i, slot = _ws_ids()
    q_start = 1 + PROJ_U_TILES
    kv_start = q_start + PROJ_Q_TILES

    @pl.when(j == 0)
    def _():
        _cast_chunk(w_ref, scr, slot, i)

    @pl.when((j >= 1) & (j < q_start))
    def _():
        u_ref[...] = _dot(a_ref[...], scr[1 - slot])
        _cast_chunk(w_ref, scr, slot, i)

    @pl.when((j >= q_start) & (j < kv_start))
    def _():
        q_ref[...] = (_dot(a_ref[...], scr[1 - slot]) * HEAD_DIM ** -0.5).astype(q_ref.dtype)
        _cast_chunk(w_ref, scr, slot, i)

    @pl.when(j >= kv_start)
    def _():
        kv_ref[...] = _dot(a_ref[...], scr[1 - slot]).astype(kv_ref.dtype)
        _cast_chunk(w_ref, scr, slot, i)


def _proj_out_spec(tm, ni, first, n_tiles):
    def index(j, i):
        row = jnp.where(j < first, 0, jnp.where(j < first + n_tiles, i, ni - 1))
        return row, jnp.clip(j - first, 0, n_tiles - 1)
    return pl.BlockSpec((tm, PROJ_TN), index)


def _proj(a, w, layer, tm=1024):
    m, k = a.shape
    ni, nj = m // tm, IN_WIDTH // PROJ_TN
    q_start = 1 + PROJ_U_TILES
    return pl.pallas_call(
        _proj_kernel,
        grid=(nj + 1, ni),
        in_specs=[_a_spec(tm, k), _w_spec(layer, k // ni, PROJ_TN, nj)],
        out_specs=[_proj_out_spec(tm, ni, 1, PROJ_U_TILES),
                   _proj_out_spec(tm, ni, q_start, PROJ_Q_TILES),
                   _proj_out_spec(tm, ni, q_start + PROJ_Q_TILES, PROJ_KV_TILES)],
        out_shape=[jax.ShapeDtypeStruct((m, POOL_WIDTH), F32),
                   jax.ShapeDtypeStruct((m, Q_WIDTH), BF16),
                   jax.ShapeDtypeStruct((m, 2 * KV_WIDTH), BF16)],
        scratch_shapes=[pltpu.VMEM((2, k, PROJ_TN), BF16)],
        compiler_params=_params("arbitrary", "arbitrary"),
        name="proj",
    )(a, w)


POOL_PAD = 16


def _pool_kernel(u_ref, w_ref, s_ref, o_ref, a_ref, b_ref):
    g = pl.program_id(1)
    zeros = jnp.zeros((POOL_PAD, POOL_GROUP), F32)
    a_ref[0:POOL_PAD, :] = zeros
    b_ref[0:POOL_PAD, :] = zeros
    body = pl.ds(POOL_PAD, SEQ)
    a_ref[body, :] = u_ref[...]
    row = lax.broadcasted_iota(jnp.int32, (SEQ, 1), 0)

    def lagged(ref, lag):
        return ref[pl.ds(POOL_PAD - lag, SEQ), :]

    for group in range(N_GROUPS):
        @pl.when(g == group)
        def _():
            src, dst = a_ref, b_ref
            for level in range(group):
                dst[body, :] = src[body, :] + lagged(src, 1 << level)
                src, dst = dst, src
            total = src[body, :] + lagged(src, 1 << group)
            inv_count = 1.0 / jnp.minimum(row + 1, POOL_WINDOWS[group]).astype(F32)
            pooled = total * inv_count - u_ref[...]
            y = _dot(pooled.astype(BF16), w_ref[0, 0].astype(BF16))
            o_ref[...] = (y * s_ref[0]).astype(o_ref.dtype)


def _pool_branch(u, w_mix, scale, layer):
    return pl.pallas_call(
        _pool_kernel,
        grid=(BATCH, N_GROUPS),
        in_specs=[
            pl.BlockSpec((SEQ, POOL_GROUP), lambda b, g: (b, g)),
            pl.BlockSpec((1, 1, POOL_GROUP, POOL_GROUP), lambda b, g: (layer, g, 0, 0)),
            pl.BlockSpec((1, 1, POOL_GROUP), lambda b, g: (layer, 0, g)),
        ],
        out_specs=pl.BlockSpec((SEQ, POOL_GROUP), lambda b, g: (b, g)),
        out_shape=jax.ShapeDtypeStruct((ROWS, POOL_WIDTH), BF16),
        scratch_shapes=[pltpu.VMEM((POOL_PAD + SEQ, POOL_GROUP), F32),
                        pltpu.VMEM((POOL_PAD + SEQ, POOL_GROUP), F32)],
        compiler_params=_params("arbitrary", "arbitrary"),
        name="pool_branch",
    )(u, w_mix, scale.reshape(DEPTH, 1, POOL_WIDTH))


BAND = 2 * BLOCK
PAIRS = Q_PER_KV // 2


def _bias_kernel(rel_ref, bucket_ref, o_ref):
    h = pl.program_id(1)
    bucket = bucket_ref[0]
    for p in range(PAIRS):
        for e in range(2):
            head = h * Q_PER_KV + 2 * p + e
            acc = jnp.full(bucket.shape, NEG, F32)
            for b in range(N_BUCKETS):
                acc = jnp.where(bucket == b, rel_ref[b, head], acc)
            r0 = ((p // 2) * 2 + e) * BAND
            c0 = (p % 2) * BLOCK
            o_ref[0, 0, r0:r0 + BAND, c0:c0 + BLOCK] = acc


def _band_bias(rel_bias, bucket_pair):
    return pl.pallas_call(
        _bias_kernel,
        grid=(2, N_KV_HEADS),
        in_specs=[
            pl.BlockSpec(memory_space=pltpu.SMEM),
            pl.BlockSpec((1, BAND, BLOCK), lambda f, h: (f, 0, 0)),
        ],
        out_specs=pl.BlockSpec((1, 1, PAIRS * BAND, 2 * BLOCK), lambda f, h: (f, h, 0, 0)),
        out_shape=jax.ShapeDtypeStruct((2, N_KV_HEADS, PAIRS * BAND, 2 * BLOCK), F32),
        compiler_params=_params("arbitrary", "arbitrary"),
        name="band_bias",
    )(rel_bias, bucket_pair)


def _bucket_tables():
    i = jnp.arange(BLOCK)[None, :]
    j = jnp.arange(BAND)[:, None]
    dist = i + BLOCK - j
    max_exact = N_BUCKETS // 2
    d = jnp.maximum(dist, 0)
    log_ratio = jnp.log(jnp.maximum(d, 1).astype(F32) / max_exact) / math.log(MAX_DISTANCE / max_exact)
    large = jnp.minimum(max_exact + (log_ratio * (N_BUCKETS - max_exact)).astype(jnp.int32), N_BUCKETS - 1)
    bucket = jnp.where(d < max_exact, d, large)
    in_window = (dist >= 0) & (dist < WINDOW)
    later = jnp.where(in_window, bucket, -1)
    first = jnp.where(j >= BLOCK, later, -1)
    return jnp.stack([first, later]).astype(jnp.int32)


N_ATTN_STEPS = ROWS // BLOCK
ADA_SIDE_ROWS = D_MODEL // N_ATTN_STEPS
N_ATTN_GROUPS = N_KV_HEADS * (PAIRS // 2)
ADA_SIDE_COLS = N_ADA // N_ATTN_GROUPS


def _attn_kernel(sink_ref, q_ref, kvp_ref, kvc_ref, bias_ref, *rest, layer, with_mod):
    if with_mod:
        c_ref, wada_ref, bada_ref, o_ref, mod_ref = rest

        @pl.when(pl.program_id(0) == 0)
        def _():
            mod_ref[...] = jnp.broadcast_to(bada_ref[0], mod_ref.shape)

        cond = _silu_bf16(c_ref[0])

        def next_mod(n):
            cols = slice(n * ADA_SIDE_COLS, (n + 1) * ADA_SIDE_COLS)
            mod_ref[:, cols] += _dot(cond, wada_ref[0, :, cols].astype(BF16))
    else:
        o_ref, = rest

        def next_mod(n):
            pass

    lane = lax.broadcasted_iota(jnp.int32, (1, 2 * HEAD_DIM), 1)
    low = lane < HEAD_DIM
    left = lax.broadcasted_iota(jnp.int32, (1, 2 * BLOCK), 1) < BLOCK
    ones = jnp.ones((16, BAND), BF16)
    nt = (((1,), (1,)), ((), ()))
    keys, values = [], []
    for slab in range(N_KV_HEADS // 2):
        c = slab * 2 * HEAD_DIM
        k_slab = jnp.concatenate([kvp_ref[:, c:c + 2 * HEAD_DIM], kvc_ref[:, c:c + 2 * HEAD_DIM]], axis=0)
        v_slab = jnp.concatenate([kvp_ref[:, KV_WIDTH + c:KV_WIDTH + c + 2 * HEAD_DIM],
                                  kvc_ref[:, KV_WIDTH + c:KV_WIDTH + c + 2 * HEAD_DIM]], axis=0)
        k_swap = pltpu.roll(k_slab, HEAD_DIM, axis=1)
        v_t = v_slab.astype(F32).T.astype(BF16)
        zero = jnp.zeros_like(k_slab)
        for hh in range(2):
            k_on_low, k_on_high = (k_slab, k_swap) if hh == 0 else (k_swap, k_slab)
            keys.append((jnp.where(low, k_on_low, zero), jnp.where(low, zero, k_on_high)))
            values.append(jnp.concatenate([v_t[hh * HEAD_DIM:(hh + 1) * HEAD_DIM], ones], axis=0))

    def scores(h, t):
        c0 = h * Q_PER_KV * HEAD_DIM + t * 2 * BLOCK
        q2 = jnp.concatenate([q_ref[:, c0:c0 + BLOCK], q_ref[:, c0 + BLOCK:c0 + 2 * BLOCK]], axis=0)
        return [lax.dot_general(keys[h][e], q2, nt, preferred_element_type=F32)
                + bias_ref[0, h, (t * 2 + e) * BAND:(t * 2 + e + 1) * BAND, :] for e in range(2)]

    def softmax(h, t, s_pair):
        probs, tails = [], []
        for e in range(2):
            head0 = h * Q_PER_KV + 4 * t + e
            sink = jnp.where(left, sink_ref[layer, head0], sink_ref[layer, head0 + 2])
            m = jnp.maximum(jnp.max(s_pair[e], axis=0, keepdims=True), sink)
            probs.append(jnp.exp(s_pair[e] - m).astype(BF16))
            tails.append(jnp.exp(sink - m))
        return probs, tails

    def finish(h, t, probs, tails):
        for pp in range(2):
            cols = slice(pp * BLOCK, (pp + 1) * BLOCK)
            p_t = jnp.concatenate([probs[0][:, cols], probs[1][:, cols]], axis=1)
            tail = jnp.concatenate([tails[0][:, cols], tails[1][:, cols]], axis=1)
            res = _dot(values[h], p_t)
            out_t = res[:HEAD_DIM] * (1.0 / (res[HEAD_DIM:HEAD_DIM + 1] + tail))
            pair_t = jnp.concatenate([out_t[:, :BLOCK], out_t[:, BLOCK:]], axis=0)
            o0 = h * Q_PER_KV * HEAD_DIM + (2 * t + pp) * BLOCK
            o_ref[:, o0:o0 + BLOCK] = pair_t.T.astype(o_ref.dtype)

    groups = [(h, t) for h in range(N_KV_HEADS) for t in range(PAIRS // 2)]
    pending = scores(*groups[0])
    ready = None
    for n, group in enumerate(groups):
        current = pending
        if n + 1 < len(groups):
            pending = scores(*groups[n + 1])
        weights = softmax(*group, current)
        next_mod(n)
        if ready is not None:
            finish(*groups[n - 1], *ready)
        ready = weights
    finish(*groups[-1], *ready)


def _attention(q, kv, sinks, bias, layer, cond_slabs=None, w_ada=None, b_ada=None):
    with_mod = cond_slabs is not None
    in_specs = [
        pl.BlockSpec(memory_space=pltpu.SMEM),
        pl.BlockSpec((BLOCK, Q_WIDTH), lambda r: (r, 0)),
        pl.BlockSpec((BLOCK, 2 * KV_WIDTH), lambda r: (jnp.maximum(r - 1, 0), 0)),
        pl.BlockSpec((BLOCK, 2 * KV_WIDTH), lambda r: (r, 0)),
        pl.BlockSpec((1, N_KV_HEADS, PAIRS * BAND, 2 * BLOCK),
                     lambda r: (jnp.minimum(r % N_BLOCKS, 1), 0, 0, 0)),
    ]
    out_specs = [pl.BlockSpec((BLOCK, Q_WIDTH), lambda r: (r, 0))]
    out_shape = [jax.ShapeDtypeStruct((ROWS, Q_WIDTH), BF16)]
    args = [sinks, q, kv, kv, bias]
    if with_mod:
        in_specs += [
            pl.BlockSpec((1, 8, ADA_SIDE_ROWS), lambda r: (r, 0, 0)),
            pl.BlockSpec((1, ADA_SIDE_ROWS, N_ADA), lambda r: (layer + 1, r, 0)),
            pl.BlockSpec((1, 1, N_ADA), lambda r: (layer + 1, 0, 0)),
        ]
        out_specs.append(pl.BlockSpec((8, N_ADA), lambda r: (0, 0)))
        out_shape.append(jax.ShapeDtypeStruct((8, N_ADA), F32))
        args += [cond_slabs, w_ada, b_ada.reshape(DEPTH, 1, N_ADA)]
    return pl.pallas_call(
        functools.partial(_attn_kernel, layer=layer, with_mod=with_mod),
        grid=(N_ATTN_STEPS,),
        in_specs=in_specs,
        out_specs=out_specs,
        out_shape=out_shape,
        compiler_params=_params("arbitrary"),
        name="swa_attention",
    )(*args)


def _merge_kernel(h_ref, yp_ref, ya_ref, wgp_ref, wga_ref, wbp_ref, wba_ref, o_ref,
                  sgp, sga, sbp, sba):
    j, i, slot = _ws_ids()

    def cast():
        _cast_chunk(wgp_ref, sgp, slot, i)
        _cast_chunk(wga_ref, sga, slot, i)
        _cast_chunk(wbp_ref, sbp, slot, i)
        _cast_chunk(wba_ref, sba, slot, i)

    @pl.when(j == 0)
    def _():
        cast()

    @pl.when(j > 0)
    def _():
        h = h_ref[...]
        gate_pool = jax.nn.sigmoid(_dot(h, sgp[1 - slot]))
        gate_attn = jax.nn.sigmoid(_dot(h, sga[1 - slot]))
        y_pool = _dot(yp_ref[...], sbp[1 - slot])
        y_attn = _dot(ya_ref[...], sba[1 - slot])
        o_ref[...] = (gate_pool * y_pool + gate_attn * y_attn).astype(o_ref.dtype)
        cast()


def _merge(h, yp, ya, w_gate, w_bp, w_ba, layer, tm=512, tn=512):
    ni, nj = ROWS // tm, D_MODEL // tn
    return pl.pallas_call(
        _merge_kernel,
        grid=(nj + 1, ni),
        in_specs=[
            _a_spec(tm, D_MODEL), _a_spec(tm, POOL_WIDTH), _a_spec(tm, Q_WIDTH),
            _w_spec(layer, D_MODEL // ni, tn, nj),
            _w_spec(layer, D_MODEL // ni, tn, nj, col_tile0=nj),
            _w_spec(layer, POOL_WIDTH // ni, tn, nj),
            _w_spec(layer, Q_WIDTH // ni, tn, nj),
        ],
        out_specs=_o_spec(tm, tn),
        out_shape=jax.ShapeDtypeStruct((ROWS, D_MODEL), BF16),
        scratch_shapes=[pltpu.VMEM((2, D_MODEL, tn), BF16), pltpu.VMEM((2, D_MODEL, tn), BF16),
                        pltpu.VMEM((2, POOL_WIDTH, tn), BF16), pltpu.VMEM((2, Q_WIDTH, tn), BF16)],
        compiler_params=_params("arbitrary", "arbitrary"),
        name="gated_merge",
    )(h, yp, ya, w_gate, w_gate, w_bp, w_ba)


def _resid_kernel(a_ref, w_ref, x_ref, g_ref, o_ref, scr):
    j, i, slot = _ws_ids()

    @pl.when(j == 0)
    def _():
        _cast_chunk(w_ref, scr, slot, i)

    @pl.when(j > 0)
    def _():
        o_ref[...] = x_ref[...] + g_ref[0] * _dot(a_ref[...], scr[1 - slot])
        _cast_chunk(w_ref, scr, slot, i)


def _resid_proj(a, w, x, gate, layer, tm, tn):
    m, k = a.shape
    n = x.shape[1]
    ni, nj = m // tm, n // tn
    per_batch = SEQ // tm
    return pl.pallas_call(
        _resid_kernel,
        grid=(nj + 1, ni),
        in_specs=[
            _a_spec(tm, k),
            _w_spec(layer, k // ni, tn, nj),
            _o_spec(tm, tn),
            pl.BlockSpec((1, 1, tn), lambda j, i: (i // per_batch, 0, jnp.maximum(j - 1, 0))),
        ],
        out_specs=_o_spec(tm, tn),
        out_shape=jax.ShapeDtypeStruct((m, n), F32),
        scratch_shapes=[pltpu.VMEM((2, k, tn), BF16)],
        compiler_params=_params("arbitrary", "arbitrary"),
        name="resid_proj",
    )(a, w, x, gate)


FF_TILE = V7X_MXU_DIM


def _swiglu_kernel(a_ref, wa_ref, wb_ref, o_ref, scr):
    j, i, slot = _ws_ids()

    def cast():
        _cast_chunk(wa_ref, scr, slot, i)
        _cast_chunk(wb_ref, scr, slot, i, col0=FF_TILE)

    @pl.when(j == 0)
    def _():
        cast()

    @pl.when(j > 0)
    def _():
        ab = _dot(a_ref[...], scr[1 - slot])
        a = ab[:, :FF_TILE]
        b = ab[:, FF_TILE:]
        o_ref[...] = ((a * jax.nn.sigmoid(a)) * b).astype(o_ref.dtype)
        cast()


def _swiglu(h2, w_ffn_in, layer, tm=2048):
    ni, nj = ROWS // tm, D_FF // FF_TILE
    return pl.pallas_call(
        _swiglu_kernel,
        grid=(nj + 1, ni),
        in_specs=[
            _a_spec(tm, D_MODEL),
            _w_spec(layer, D_MODEL // ni, FF_TILE, nj),
            _w_spec(layer, D_MODEL // ni, FF_TILE, nj, col_tile0=nj),
        ],
        out_specs=_o_spec(tm, FF_TILE),
        out_shape=jax.ShapeDtypeStruct((ROWS, D_FF), BF16),
        scratch_shapes=[pltpu.VMEM((2, D_MODEL, 2 * FF_TILE), BF16)],
        compiler_params=_params("arbitrary", "arbitrary"),
        name="swiglu_up",
    )(h2, w_ffn_in, w_ffn_in)


def kernel(x, c, w_ada, b_ada, norm1, w_in, w_pool_mix, pool_scale, sinks, rel_bias,
           w_branch_pool, w_branch_attn, w_gate, w_out, norm2, w_ffn_in, w_ffn_out,
           final_norm):
    c_pad = jnp.pad(c, ((0, 8 - BATCH), (0, 0)))
    cond_slabs = c_pad.reshape(8, N_ATTN_STEPS, ADA_SIDE_ROWS).transpose(1, 0, 2)
    mod = _ada_first(c_pad, w_ada, b_ada)
    bias = _band_bias(rel_bias, _bucket_tables())
    xr = x.reshape(ROWS, D_MODEL)

    for l in range(DEPTH):
        sh1, sc1, g1, sh2, sc2, g2 = [m.reshape(BATCH, 1, D_MODEL)
                                      for m in jnp.split(mod[:BATCH], N_MOD, axis=-1)]
        h = _mod_norm(xr, norm1, l, sc1, sh1)
        u, q, kv = _proj(h, w_in, l)
        yp = _pool_branch(u, w_pool_mix, pool_scale, l)
        if l + 1 < DEPTH:
            ya, mod = _attention(q, kv, sinks, bias, l, cond_slabs, w_ada, b_ada)
        else:
            ya, = _attention(q, kv, sinks, bias, l)
        merged = _merge(h, yp, ya, w_gate, w_branch_pool, w_branch_attn, l)
        xr = _resid_proj(merged, w_out, xr, g1, l, tm=512, tn=1024)

        h2 = _mod_norm(xr, norm2, l, sc2, sh2)
        act = _swiglu(h2, w_ffn_in, l)
        xr = _resid_proj(act, w_ffn_out, xr, g2, l, tm=512, tn=512)

    return _final_norm(xr, final_norm).reshape(BATCH, SEQ, D_MODEL)
```

```python
import functools
import math

import jax
import jax.numpy as jnp
from jax import lax
from jax.experimental import pallas as pl
from jax.experimental.pallas import tpu as pltpu

D_MODEL = 4096
BATCH = 4
SEQ = 2048
ROWS = BATCH * SEQ
DEPTH = 2
POOL_WINDOWS = (2, 4, 8, 16)
N_GROUPS = len(POOL_WINDOWS)
POOL_WIDTH = D_MODEL // 2
POOL_GROUP = POOL_WIDTH // N_GROUPS
HEAD_DIM = 64
N_Q_HEADS = (D_MODEL // 2) // HEAD_DIM
N_KV_HEADS = N_Q_HEADS // 8
Q_PER_KV = N_Q_HEADS // N_KV_HEADS
Q_WIDTH = N_Q_HEADS * HEAD_DIM
KV_WIDTH = N_KV_HEADS * HEAD_DIM
WINDOW = 128
BLOCK = 128
N_BLOCKS = SEQ // BLOCK
N_BUCKETS = 32
MAX_DISTANCE = 128
IN_WIDTH = POOL_WIDTH + Q_WIDTH + 2 * KV_WIDTH
D_FF = -(-(8 * D_MODEL) // (3 * 256)) * 256
N_MOD = 6
EPS = 1e-6

V7X_VMEM_BYTES = 64 * 1024 * 1024
VMEM_LIMIT = V7X_VMEM_BYTES - 8 * 1024 * 1024
V7X_MXU_DIM = 256

NEG = -1e30

BF16 = jnp.bfloat16
F32 = jnp.float32


def _params(*sem):
    return pltpu.CompilerParams(dimension_semantics=sem, vmem_limit_bytes=VMEM_LIMIT)


def _dot(a, b):
    return jnp.dot(a, b, preferred_element_type=F32)


ADA_ROWS = 128
N_ADA = N_MOD * D_MODEL


def _silu_bf16(c):
    return (c * jax.nn.sigmoid(c)).astype(BF16)


def _ada_kernel(c_ref, w_ref, b_ref, o_ref):
    k = pl.program_id(0)

    @pl.when(k == 0)
    def _():
        o_ref[...] = jnp.broadcast_to(b_ref[0], o_ref.shape)

    o_ref[...] += _dot(_silu_bf16(c_ref[...]), w_ref[0].astype(BF16))


def _ada_first(c_pad, w_ada, b_ada):
    return pl.pallas_call(
        _ada_kernel,
        grid=(D_MODEL // ADA_ROWS,),
        in_specs=[
            pl.BlockSpec((8, ADA_ROWS), lambda k: (0, k)),
            pl.BlockSpec((1, ADA_ROWS, N_ADA), lambda k: (0, k, 0)),
            pl.BlockSpec((1, 1, N_ADA), lambda k: (0, 0, 0)),
        ],
        out_specs=pl.BlockSpec((8, N_ADA), lambda k: (0, 0)),
        out_shape=jax.ShapeDtypeStruct((8, N_ADA), F32),
        compiler_params=_params("arbitrary"),
        name="ada_mod",
    )(c_pad, w_ada, b_ada.reshape(DEPTH, 1, N_ADA))


NORM_CHUNK = 64


def _norm_kernel(x_ref, g_ref, sc_ref, sh_ref, o_ref):
    gain = g_ref[0] * (1.0 + sc_ref[0])
    shift = sh_ref[0]

    def chunk(c, carry):
        rows = pl.ds(pl.multiple_of(c * NORM_CHUNK, NORM_CHUNK), NORM_CHUNK)
        x = x_ref[rows, :]
        r = lax.rsqrt(jnp.mean(x * x, axis=-1, keepdims=True) + EPS)
        o_ref[rows, :] = ((x * r) * gain + shift).astype(o_ref.dtype)
        return carry

    lax.fori_loop(0, x_ref.shape[0] // NORM_CHUNK, chunk, 0)


def _mod_norm(x, gain, layer, scale, shift):
    tm = 512
    per_batch = SEQ // tm
    row = lambda i: (i // per_batch, 0, 0)
    return pl.pallas_call(
        _norm_kernel,
        grid=(ROWS // tm,),
        in_specs=[
            pl.BlockSpec((tm, D_MODEL), lambda i: (i, 0)),
            pl.BlockSpec((1, 1, D_MODEL), lambda i: (layer, 0, 0)),
            pl.BlockSpec((1, 1, D_MODEL), row),
            pl.BlockSpec((1, 1, D_MODEL), row),
        ],
        out_specs=pl.BlockSpec((tm, D_MODEL), lambda i: (i, 0)),
        out_shape=jax.ShapeDtypeStruct((ROWS, D_MODEL), BF16),
        compiler_params=_params("arbitrary"),
        name="mod_norm",
    )(x, gain.reshape(DEPTH, 1, D_MODEL), scale, shift)


def _final_norm_kernel(x_ref, g_ref, o_ref):
    gain = g_ref[...]

    def chunk(c, carry):
        rows = pl.ds(pl.multiple_of(c * NORM_CHUNK, NORM_CHUNK), NORM_CHUNK)
        x = x_ref[rows, :]
        r = lax.rsqrt(jnp.mean(x * x, axis=-1, keepdims=True) + EPS)
        o_ref[rows, :] = x * r * gain
        return carry

    lax.fori_loop(0, x_ref.shape[0] // NORM_CHUNK, chunk, 0)


def _final_norm(x, gain):
    tm = 512
    return pl.pallas_call(
        _final_norm_kernel,
        grid=(ROWS // tm,),
        in_specs=[
            pl.BlockSpec((tm, D_MODEL), lambda i: (i, 0)),
            pl.BlockSpec((1, D_MODEL), lambda i: (0, 0)),
        ],
        out_specs=pl.BlockSpec((tm, D_MODEL), lambda i: (i, 0)),
        out_shape=jax.ShapeDtypeStruct((ROWS, D_MODEL), F32),
        compiler_params=_params("arbitrary"),
        name="final_norm",
    )(x, gain.reshape(1, D_MODEL))


def _cast_chunk(w_ref, scr, slot, step, col0=0):
    rows, cols = w_ref.shape[1], w_ref.shape[2]
    r0 = pl.multiple_of(step * rows, rows)
    scr[slot, pl.ds(r0, rows), col0:col0 + cols] = w_ref[0].astype(BF16)


def _ws_ids():
    j = pl.program_id(0)
    return j, pl.program_id(1), j % 2


def _w_spec(layer, rows, cols, n_tiles, col_tile0=0, row_chunk0=0):
    return pl.BlockSpec((1, rows, cols),
                        lambda j, i: (layer, row_chunk0 + i, col_tile0 + jnp.minimum(j, n_tiles - 1)))


def _a_spec(tm, k, col_block=0):
    return pl.BlockSpec((tm, k), lambda j, i: (jnp.where(j > 0, i, 0), col_block))


def _o_spec(tm, tn):
    return pl.BlockSpec((tm, tn), lambda j, i: (jnp.where(j > 0, i, 0), jnp.maximum(j - 1, 0)))


PROJ_TN = 512
PROJ_U_TILES = POOL_WIDTH // PROJ_TN
PROJ_Q_TILES = Q_WIDTH // PROJ_TN
PROJ_KV_TILES = 2 * KV_WIDTH // PROJ_TN


def _proj_kernel(a_ref, w_ref, u_ref, q_ref, kv_ref, scr):
    j, i, slot = _ws_ids()
    q_start = 1 + PROJ_U_TILES
    kv_start = q_start + PROJ_Q_TILES

    @pl.when(j == 0)
    def _():
        _cast_chunk(w_ref, scr, slot, i)

    @pl.when((j >= 1) & (j < q_start))
    def _():
        u_ref[...] = _dot(a_ref[...], scr[1 - slot])
        _cast_chunk(w_ref, scr, slot, i)

    @pl.when((j >= q_start) & (j < kv_start))
    def _():
        q_ref[...] = (_dot(a_ref[...], scr[1 - slot]) * HEAD_DIM ** -0.5).astype(q_ref.dtype)
        _cast_chunk(w_ref, scr, slot, i)

    @pl.when(j >= kv_start)
    def _():
        kv_ref[...] = _dot(a_ref[...], scr[1 - slot]).astype(kv_ref.dtype)
        _cast_chunk(w_ref, scr, slot, i)


def _proj_out_spec(tm, ni, first, n_tiles):
    def index(j, i):
        row = jnp.where(j < first, 0, jnp.where(j < first + n_tiles, i, ni - 1))
        return row, jnp.clip(j - first, 0, n_tiles - 1)
    return pl.BlockSpec((tm, PROJ_TN), index)


def _proj(a, w, layer, tm=1024):
    m, k = a.shape
    ni, nj = m // tm, IN_WIDTH // PROJ_TN
    q_start = 1 + PROJ_U_TILES
    return pl.pallas_call(
        _proj_kernel,
        grid=(nj + 1, ni),
        in_specs=[_a_spec(tm, k), _w_spec(layer, k // ni, PROJ_TN, nj)],
        out_specs=[_proj_out_spec(tm, ni, 1, PROJ_U_TILES),
                   _proj_out_spec(tm, ni, q_start, PROJ_Q_TILES),
                   _proj_out_spec(tm, ni, q_start + PROJ_Q_TILES, PROJ_KV_TILES)],
        out_shape=[jax.ShapeDtypeStruct((m, POOL_WIDTH), F32),
                   jax.ShapeDtypeStruct((m, Q_WIDTH), BF16),
                   jax.ShapeDtypeStruct((m, 2 * KV_WIDTH), BF16)],
        scratch_shapes=[pltpu.VMEM((2, k, PROJ_TN), BF16)],
        compiler_params=_params("arbitrary", "arbitrary"),
        name="proj",
    )(a, w)


POOL_PAD = 16


def _pool_kernel(u_ref, w_ref, s_ref, o_ref, a_ref, b_ref):
    g = pl.program_id(1)
    zeros = jnp.zeros((POOL_PAD, POOL_GROUP), F32)
    a_ref[0:POOL_PAD, :] = zeros
    b_ref[0:POOL_PAD, :] = zeros
    body = pl.ds(POOL_PAD, SEQ)
    a_ref[body, :] = u_ref[...]
    row = lax.broadcasted_iota(jnp.int32, (SEQ, 1), 0)

    def lagged(ref, lag):
        return ref[pl.ds(POOL_PAD - lag, SEQ), :]

    for group in range(N_GROUPS):
        @pl.when(g == group)
        def _():
            src, dst = a_ref, b_ref
            for level in range(group):
                dst[body, :] = src[body, :] + lagged(src, 1 << level)
                src, dst = dst, src
            total = src[body, :] + lagged(src, 1 << group)
            inv_count = 1.0 / jnp.minimum(row + 1, POOL_WINDOWS[group]).astype(F32)
            pooled = total * inv_count - u_ref[...]
            y = _dot(pooled.astype(BF16), w_ref[0, 0].astype(BF16))
            o_ref[...] = (y * s_ref[0]).astype(o_ref.dtype)


def _pool_branch(u, w_mix, scale, layer):
    return pl.pallas_call(
        _pool_kernel,
        grid=(BATCH, N_GROUPS),
        in_specs=[
            pl.BlockSpec((SEQ, POOL_GROUP), lambda b, g: (b, g)),
            pl.BlockSpec((1, 1, POOL_GROUP, POOL_GROUP), lambda b, g: (layer, g, 0, 0)),
            pl.BlockSpec((1, 1, POOL_GROUP), lambda b, g: (layer, 0, g)),
        ],
        out_specs=pl.BlockSpec((SEQ, POOL_GROUP), lambda b, g: (b, g)),
        out_shape=jax.ShapeDtypeStruct((ROWS, POOL_WIDTH), BF16),
        scratch_shapes=[pltpu.VMEM((POOL_PAD + SEQ, POOL_GROUP), F32),
                        pltpu.VMEM((POOL_PAD + SEQ, POOL_GROUP), F32)],
        compiler_params=_params("arbitrary", "arbitrary"),
        name="pool_branch",
    )(u, w_mix, scale.reshape(DEPTH, 1, POOL_WIDTH))


BAND = 2 * BLOCK
PAIRS = Q_PER_KV // 2


def _bias_kernel(rel_ref, bucket_ref, o_ref):
    h = pl.program_id(1)
    bucket = bucket_ref[0]
    for p in range(PAIRS):
        for e in range(2):
            head = h * Q_PER_KV + 2 * p + e
            acc = jnp.full(bucket.shape, NEG, F32)
            for b in range(N_BUCKETS):
                acc = jnp.where(bucket == b, rel_ref[b, head], acc)
            r0 = ((p // 2) * 2 + e) * BAND
            c0 = (p % 2) * BLOCK
            o_ref[0, 0, r0:r0 + BAND, c0:c0 + BLOCK] = acc


def _band_bias(rel_bias, bucket_pair):
    return pl.pallas_call(
        _bias_kernel,
        grid=(2, N_KV_HEADS),
        in_specs=[
            pl.BlockSpec(memory_space=pltpu.SMEM),
            pl.BlockSpec((1, BAND, BLOCK), lambda f, h: (f, 0, 0)),
        ],
        out_specs=pl.BlockSpec((1, 1, PAIRS * BAND, 2 * BLOCK), lambda f, h: (f, h, 0, 0)),
        out_shape=jax.ShapeDtypeStruct((2, N_KV_HEADS, PAIRS * BAND, 2 * BLOCK), F32),
        compiler_params=_params("arbitrary", "arbitrary"),
        name="band_bias",
    )(rel_bias, bucket_pair)


def _bucket_tables():
    i = jnp.arange(BLOCK)[None, :]
    j = jnp.arange(BAND)[:, None]
    dist = i + BLOCK - j
    max_exact = N_BUCKETS // 2
    d = jnp.maximum(dist, 0)
    log_ratio = jnp.log(jnp.maximum(d, 1).astype(F32) / max_exact) / math.log(MAX_DISTANCE / max_exact)
    large = jnp.minimum(max_exact + (log_ratio * (N_BUCKETS - max_exact)).astype(jnp.int32), N_BUCKETS - 1)
    bucket = jnp.where(d < max_exact, d, large)
    in_window = (dist >= 0) & (dist < WINDOW)
    later = jnp.where(in_window, bucket, -1)
    first = jnp.where(j >= BLOCK, later, -1)
    return jnp.stack([first, later]).astype(jnp.int32)


N_ATTN_STEPS = ROWS // BLOCK
ADA_SIDE_ROWS = D_MODEL // N_ATTN_STEPS
N_ATTN_GROUPS = N_KV_HEADS * (PAIRS // 2)
ADA_SIDE_COLS = N_ADA // N_ATTN_GROUPS


def _attn_kernel(sink_ref, q_ref, kvp_ref, kvc_ref, bias_ref, *rest, layer, with_mod):
    if with_mod:
        c_ref, wada_ref, bada_ref, o_ref, mod_ref = rest

        @pl.when(pl.program_id(0) == 0)
        def _():
            mod_ref[...] = jnp.broadcast_to(bada_ref[0], mod_ref.shape)

        cond = _silu_bf16(c_ref[0])

        def next_mod(n):
            cols = slice(n * ADA_SIDE_COLS, (n + 1) * ADA_SIDE_COLS)
            mod_ref[:, cols] += _dot(cond, wada_ref[0, :, cols].astype(BF16))
    else:
        o_ref, = rest

        def next_mod(n):
            pass

    lane = lax.broadcasted_iota(jnp.int32, (1, 2 * HEAD_DIM), 1)
    low = lane < HEAD_DIM
    left = lax.broadcasted_iota(jnp.int32, (1, 2 * BLOCK), 1) < BLOCK
    ones = jnp.ones((16, BAND), BF16)
    nt = (((1,), (1,)), ((), ()))
    keys, values = [], []
    for slab in range(N_KV_HEADS // 2):
        c = slab * 2 * HEAD_DIM
        k_slab = jnp.concatenate([kvp_ref[:, c:c + 2 * HEAD_DIM], kvc_ref[:, c:c + 2 * HEAD_DIM]], axis=0)
        v_slab = jnp.concatenate([kvp_ref[:, KV_WIDTH + c:KV_WIDTH + c + 2 * HEAD_DIM],
                                  kvc_ref[:, KV_WIDTH + c:KV_WIDTH + c + 2 * HEAD_DIM]], axis=0)
        k_swap = pltpu.roll(k_slab, HEAD_DIM, axis=1)
        v_t = v_slab.astype(F32).T.astype(BF16)
        zero = jnp.zeros_like(k_slab)
        for hh in range(2):
            k_on_low, k_on_high = (k_slab, k_swap) if hh == 0 else (k_swap, k_slab)
            keys.append((jnp.where(low, k_on_low, zero), jnp.where(low, zero, k_on_high)))
            values.append(jnp.concatenate([v_t[hh * HEAD_DIM:(hh + 1) * HEAD_DIM], ones], axis=0))

    def scores(h, t):
        c0 = h * Q_PER_KV * HEAD_DIM + t * 2 * BLOCK
        q2 = jnp.concatenate([q_ref[:, c0:c0 + BLOCK], q_ref[:, c0 + BLOCK:c0 + 2 * BLOCK]], axis=0)
        return [lax.dot_general(keys[h][e], q2, nt, preferred_element_type=F32)
                + bias_ref[0, h, (t * 2 + e) * BAND:(t * 2 + e + 1) * BAND, :] for e in range(2)]

    def softmax(h, t, s_pair):
        probs, tails = [], []
        for e in range(2):
            head0 = h * Q_PER_KV + 4 * t + e
            sink = jnp.where(left, sink_ref[layer, head0], sink_ref[layer, head0 + 2])
            m = jnp.maximum(jnp.max(s_pair[e], axis=0, keepdims=True), sink)
            probs.append(jnp.exp(s_pair[e] - m).astype(BF16))
            tails.append(jnp.exp(sink - m))
        return probs, tails

    def finish(h, t, probs, tails):
        for pp in range(2):
            cols = slice(pp * BLOCK, (pp + 1) * BLOCK)
            p_t = jnp.concatenate([probs[0][:, cols], probs[1][:, cols]], axis=1)
            tail = jnp.concatenate([tails[0][:, cols], tails[1][:, cols]], axis=1)
            res = _dot(values[h], p_t)
            out_t = res[:HEAD_DIM] * (1.0 / (res[HEAD_DIM:HEAD_DIM + 1] + tail))
            pair_t = jnp.concatenate([out_t[:, :BLOCK], out_t[:, BLOCK:]], axis=0)
            o0 = h * Q_PER_KV * HEAD_DIM + (2 * t + pp) * BLOCK
            o_ref[:, o0:o0 + BLOCK] = pair_t.T.astype(o_ref.dtype)

    groups = [(h, t) for h in range(N_KV_HEADS) for t in range(PAIRS // 2)]
    pending = scores(*groups[0])
    ready = None
    for n, group in enumerate(groups):
        current = pending
        if n + 1 < len(groups):
            pending = scores(*groups[n + 1])
        weights = softmax(*group, current)
        next_mod(n)
        if ready is not None:
            finish(*groups[n - 1], *ready)
        ready = weights
    finish(*groups[-1], *ready)


def _attention(q, kv, sinks, bias, layer, cond_slabs=None, w_ada=None, b_ada=None):
    with_mod = cond_slabs is not None
    in_specs = [
        pl.BlockSpec(memory_space=pltpu.SMEM),
        pl.BlockSpec((BLOCK, Q_WIDTH), lambda r: (r, 0)),
        pl.BlockSpec((BLOCK, 2 * KV_WIDTH), lambda r: (jnp.maximum(r - 1, 0), 0)),
        pl.BlockSpec((BLOCK, 2 * KV_WIDTH), lambda r: (r, 0)),
        pl.BlockSpec((1, N_KV_HEADS, PAIRS * BAND, 2 * BLOCK),
                     lambda r: (jnp.minimum(r % N_BLOCKS, 1), 0, 0, 0)),
    ]
    out_specs = [pl.BlockSpec((BLOCK, Q_WIDTH), lambda r: (r, 0))]
    out_shape = [jax.ShapeDtypeStruct((ROWS, Q_WIDTH), BF16)]
    args = [sinks, q, kv, kv, bias]
    if with_mod:
        in_specs += [
            pl.BlockSpec((1, 8, ADA_SIDE_ROWS), lambda r: (r, 0, 0)),
            pl.BlockSpec((1, ADA_SIDE_ROWS, N_ADA), lambda r: (layer + 1, r, 0)),
            pl.BlockSpec((1, 1, N_ADA), lambda r: (layer + 1, 0, 0)),
        ]
        out_specs.append(pl.BlockSpec((8, N_ADA), lambda r: (0, 0)))
        out_shape.append(jax.ShapeDtypeStruct((8, N_ADA), F32))
        args += [cond_slabs, w_ada, b_ada.reshape(DEPTH, 1, N_ADA)]
    return pl.pallas_call(
        functools.partial(_attn_kernel, layer=layer, with_mod=with_mod),
        grid=(N_ATTN_STEPS,),
        in_specs=in_specs,
        out_specs=out_specs,
        out_shape=out_shape,
        compiler_params=_params("arbitrary"),
        name="swa_attention",
    )(*args)


def _merge_kernel(h_ref, yp_ref, ya_ref, wgp_ref, wga_ref, wbp_ref, wba_ref, o_ref,
                  sgp, sga, sbp, sba):
    j, i, slot = _ws_ids()

    def cast():
        _cast_chunk(wgp_ref, sgp, slot, i)
        _cast_chunk(wga_ref, sga, slot, i)
        _cast_chunk(wbp_ref, sbp, slot, i)
        _cast_chunk(wba_ref, sba, slot, i)

    @pl.when(j == 0)
    def _():
        cast()

    @pl.when(j > 0)
    def _():
        h = h_ref[...]
        gate_pool = jax.nn.sigmoid(_dot(h, sgp[1 - slot]))
        gate_attn = jax.nn.sigmoid(_dot(h, sga[1 - slot]))
        y_pool = _dot(yp_ref[...], sbp[1 - slot])
        y_attn = _dot(ya_ref[...], sba[1 - slot])
        o_ref[...] = (gate_pool * y_pool + gate_attn * y_attn).astype(o_ref.dtype)
        cast()


def _merge(h, yp, ya, w_gate, w_bp, w_ba, layer, tm=512, tn=512):
    ni, nj = ROWS // tm, D_MODEL // tn
    return pl.pallas_call(
        _merge_kernel,
        grid=(nj + 1, ni),
        in_specs=[
            _a_spec(tm, D_MODEL), _a_spec(tm, POOL_WIDTH), _a_spec(tm, Q_WIDTH),
            _w_spec(layer, D_MODEL // ni, tn, nj),
            _w_spec(layer, D_MODEL // ni, tn, nj, col_tile0=nj),
            _w_spec(layer, POOL_WIDTH // ni, tn, nj),
            _w_spec(layer, Q_WIDTH // ni, tn, nj),
        ],
        out_specs=_o_spec(tm, tn),
        out_shape=jax.ShapeDtypeStruct((ROWS, D_MODEL), BF16),
        scratch_shapes=[pltpu.VMEM((2, D_MODEL, tn), BF16), pltpu.VMEM((2, D_MODEL, tn), BF16),
                        pltpu.VMEM((2, POOL_WIDTH, tn), BF16), pltpu.VMEM((2, Q_WIDTH, tn), BF16)],
        compiler_params=_params("arbitrary", "arbitrary"),
        name="gated_merge",
    )(h, yp, ya, w_gate, w_gate, w_bp, w_ba)


def _resid_kernel(a_ref, w_ref, x_ref, g_ref, o_ref, scr):
    j, i, slot = _ws_ids()

    @pl.when(j == 0)
    def _():
        _cast_chunk(w_ref, scr, slot, i)

    @pl.when(j > 0)
    def _():
        o_ref[...] = x_ref[...] + g_ref[0] * _dot(a_ref[...], scr[1 - slot])
        _cast_chunk(w_ref, scr, slot, i)


def _resid_proj(a, w, x, gate, layer, tm, tn):
    m, k = a.shape
    n = x.shape[1]
    ni, nj = m // tm, n // tn
    per_batch = SEQ // tm
    return pl.pallas_call(
        _resid_kernel,
        grid=(nj + 1, ni),
        in_specs=[
            _a_spec(tm, k),
            _w_spec(layer, k // ni, tn, nj),
            _o_spec(tm, tn),
            pl.BlockSpec((1, 1, tn), lambda j, i: (i // per_batch, 0, jnp.maximum(j - 1, 0))),
        ],
        out_specs=_o_spec(tm, tn),
        out_shape=jax.ShapeDtypeStruct((m, n), F32),
        scratch_shapes=[pltpu.VMEM((2, k, tn), BF16)],
        compiler_params=_params("arbitrary", "arbitrary"),
        name="resid_proj",
    )(a, w, x, gate)


FF_TILE = V7X_MXU_DIM
SWIGLU_ROWS = 512


def _swiglu_kernel(a_ref, wa_ref, wb_ref, o_ref, scr):
    j, i, slot = _ws_ids()

    def cast():
        _cast_chunk(wa_ref, scr, slot, i)
        _cast_chunk(wb_ref, scr, slot, i, col0=FF_TILE)

    @pl.when(j == 0)
    def _():
        cast()

    @pl.when(j > 0)
    def _():
        w = scr[1 - slot]
        for r0 in range(0, a_ref.shape[0], SWIGLU_ROWS):
            ab = _dot(a_ref[r0:r0 + SWIGLU_ROWS, :], w)
            a = ab[:, :FF_TILE]
            b = ab[:, FF_TILE:]
            o_ref[r0:r0 + SWIGLU_ROWS, :] = ((a * jax.nn.sigmoid(a)) * b).astype(o_ref.dtype)
        cast()


def _swiglu(h2, w_ffn_in, layer, tm=2048):
    ni, nj = ROWS // tm, D_FF // FF_TILE
    return pl.pallas_call(
        _swiglu_kernel,
        grid=(nj + 1, ni),
        in_specs=[
            _a_spec(tm, D_MODEL),
            _w_spec(layer, D_MODEL // ni, FF_TILE, nj),
            _w_spec(layer, D_MODEL // ni, FF_TILE, nj, col_tile0=nj),
        ],
        out_specs=_o_spec(tm, FF_TILE),
        out_shape=jax.ShapeDtypeStruct((ROWS, D_FF), BF16),
        scratch_shapes=[pltpu.VMEM((2, D_MODEL, 2 * FF_TILE), BF16)],
        compiler_params=_params("arbitrary", "arbitrary"),
        name="swiglu_up",
    )(h2, w_ffn_in, w_ffn_in)


def kernel(x, c, w_ada, b_ada, norm1, w_in, w_pool_mix, pool_scale, sinks, rel_bias,
           w_branch_pool, w_branch_attn, w_gate, w_out, norm2, w_ffn_in, w_ffn_out,
           final_norm):
    c_pad = jnp.pad(c, ((0, 8 - BATCH), (0, 0)))
    cond_slabs = c_pad.reshape(8, N_ATTN_STEPS, ADA_SIDE_ROWS).transpose(1, 0, 2)
    mod = _ada_first(c_pad, w_ada, b_ada)
    bias = _band_bias(rel_bias, _bucket_tables())
    xr = x.reshape(ROWS, D_MODEL)

    for l in range(DEPTH):
        sh1, sc1, g1, sh2, sc2, g2 = [m.reshape(BATCH, 1, D_MODEL)
                                      for m in jnp.split(mod[:BATCH], N_MOD, axis=-1)]
        h = _mod_norm(xr, norm1, l, sc1, sh1)
        u, q, kv = _proj(h, w_in, l)
        yp = _pool_branch(u, w_pool_mix, pool_scale, l)
        if l + 1 < DEPTH:
            ya, mod = _attention(q, kv, sinks, bias, l, cond_slabs, w_ada, b_ada)
        else:
            ya, = _attention(q, kv, sinks, bias, l)
        merged = _merge(h, yp, ya, w_gate, w_branch_pool, w_branch_attn, l)
        xr = _resid_proj(merged, w_out, xr, g1, l, tm=512, tn=1024)

        h2 = _mod_norm(xr, norm2, l, sc2, sh2)
        act = _swiglu(h2, w_ffn_in, l)
        xr = _resid_proj(act, w_ffn_out, xr, g2, l, tm=512, tn=512)

    return _final_norm(xr, final_norm).reshape(BATCH, SEQ, D_MODEL)
```

```python
import functools
import math

import jax
import jax.numpy as jnp
from jax import lax
from jax.experimental import pallas as pl
from jax.experimental.pallas import tpu as pltpu

D_MODEL = 4096
BATCH = 4
SEQ = 2048
ROWS = BATCH * SEQ
DEPTH = 2
POOL_WINDOWS = (2, 4, 8, 16)
N_GROUPS = len(POOL_WINDOWS)
POOL_WIDTH = D_MODEL // 2
POOL_GROUP = POOL_WIDTH // N_GROUPS
HEAD_DIM = 64
N_Q_HEADS = (D_MODEL // 2) // HEAD_DIM
N_KV_HEADS = N_Q_HEADS // 8
Q_PER_KV = N_Q_HEADS // N_KV_HEADS
Q_WIDTH = N_Q_HEADS * HEAD_DIM
KV_WIDTH = N_KV_HEADS * HEAD_DIM
WINDOW = 128
BLOCK = 128
N_BLOCKS = SEQ // BLOCK
N_BUCKETS = 32
MAX_DISTANCE = 128
IN_WIDTH = POOL_WIDTH + Q_WIDTH + 2 * KV_WIDTH
D_FF = -(-(8 * D_MODEL) // (3 * 256)) * 256
N_MOD = 6
EPS = 1e-6

V7X_VMEM_BYTES = 64 * 1024 * 1024
VMEM_LIMIT = V7X_VMEM_BYTES - 8 * 1024 * 1024
V7X_MXU_DIM = 256

NEG = -1e30

BF16 = jnp.bfloat16
F32 = jnp.float32


def _params(*sem):
    return pltpu.CompilerParams(dimension_semantics=sem, vmem_limit_bytes=VMEM_LIMIT)


def _dot(a, b):
    return jnp.dot(a, b, preferred_element_type=F32)


ADA_ROWS = 128
N_ADA = N_MOD * D_MODEL


def _silu_bf16(c):
    return (c * jax.nn.sigmoid(c)).astype(BF16)


def _ada_kernel(c_ref, w_ref, b_ref, o_ref):
    k = pl.program_id(0)

    @pl.when(k == 0)
    def _():
        o_ref[...] = jnp.broadcast_to(b_ref[0], o_ref.shape)

    o_ref[...] += _dot(_silu_bf16(c_ref[...]), w_ref[0].astype(BF16))


def _ada_first(c_pad, w_ada, b_ada):
    return pl.pallas_call(
        _ada_kernel,
        grid=(D_MODEL // ADA_ROWS,),
        in_specs=[
            pl.BlockSpec((8, ADA_ROWS), lambda k: (0, k)),
            pl.BlockSpec((1, ADA_ROWS, N_ADA), lambda k: (0, k, 0)),
            pl.BlockSpec((1, 1, N_ADA), lambda k: (0, 0, 0)),
        ],
        out_specs=pl.BlockSpec((8, N_ADA), lambda k: (0, 0)),
        out_shape=jax.ShapeDtypeStruct((8, N_ADA), F32),
        compiler_params=_params("arbitrary"),
        name="ada_mod",
    )(c_pad, w_ada, b_ada.reshape(DEPTH, 1, N_ADA))


NORM_CHUNK = 64


def _norm_kernel(x_ref, g_ref, sc_ref, sh_ref, o_ref):
    gain = g_ref[0] * (1.0 + sc_ref[0])
    shift = sh_ref[0]

    def chunk(c, carry):
        rows = pl.ds(pl.multiple_of(c * NORM_CHUNK, NORM_CHUNK), NORM_CHUNK)
        x = x_ref[rows, :]
        r = lax.rsqrt(jnp.mean(x * x, axis=-1, keepdims=True) + EPS)
        o_ref[rows, :] = ((x * r) * gain + shift).astype(o_ref.dtype)
        return carry

    lax.fori_loop(0, x_ref.shape[0] // NORM_CHUNK, chunk, 0)


def _mod_norm(x, gain, layer, scale, shift):
    tm = 1024
    per_batch = SEQ // tm
    row = lambda i: (i // per_batch, 0, 0)
    return pl.pallas_call(
        _norm_kernel,
        grid=(ROWS // tm,),
        in_specs=[
            pl.BlockSpec((tm, D_MODEL), lambda i: (i, 0)),
            pl.BlockSpec((1, 1, D_MODEL), lambda i: (layer, 0, 0)),
            pl.BlockSpec((1, 1, D_MODEL), row),
            pl.BlockSpec((1, 1, D_MODEL), row),
        ],
        out_specs=pl.BlockSpec((tm, D_MODEL), lambda i: (i, 0)),
        out_shape=jax.ShapeDtypeStruct((ROWS, D_MODEL), BF16),
        compiler_params=_params("arbitrary"),
        name="mod_norm",
    )(x, gain.reshape(DEPTH, 1, D_MODEL), scale, shift)


def _final_norm_kernel(x_ref, g_ref, o_ref):
    gain = g_ref[...]

    def chunk(c, carry):
        rows = pl.ds(pl.multiple_of(c * NORM_CHUNK, NORM_CHUNK), NORM_CHUNK)
        x = x_ref[rows, :]
        r = lax.rsqrt(jnp.mean(x * x, axis=-1, keepdims=True) + EPS)
        o_ref[rows, :] = x * r * gain
        return carry

    lax.fori_loop(0, x_ref.shape[0] // NORM_CHUNK, chunk, 0)


def _final_norm(x, gain):
    tm = 512
    return pl.pallas_call(
        _final_norm_kernel,
        grid=(ROWS // tm,),
        in_specs=[
            pl.BlockSpec((tm, D_MODEL), lambda i: (i, 0)),
            pl.BlockSpec((1, D_MODEL), lambda i: (0, 0)),
        ],
        out_specs=pl.BlockSpec((tm, D_MODEL), lambda i: (i, 0)),
        out_shape=jax.ShapeDtypeStruct((ROWS, D_MODEL), F32),
        compiler_params=_params("arbitrary"),
        name="final_norm",
    )(x, gain.reshape(1, D_MODEL))


def _cast_chunk(w_ref, scr, slot, step, col0=0):
    rows, cols = w_ref.shape[1], w_ref.shape[2]
    r0 = pl.multiple_of(step * rows, rows)
    scr[slot, pl.ds(r0, rows), col0:col0 + cols] = w_ref[0].astype(BF16)


def _ws_ids():
    j = pl.program_id(0)
    return j, pl.program_id(1), j % 2


def _w_spec(layer, rows, cols, n_tiles, col_tile0=0, row_chunk0=0):
    return pl.BlockSpec((1, rows, cols),
                        lambda j, i: (layer, row_chunk0 + i, col_tile0 + jnp.minimum(j, n_tiles - 1)))


def _a_spec(tm, k, col_block=0):
    return pl.BlockSpec((tm, k), lambda j, i: (jnp.where(j > 0, i, 0), col_block))


def _o_spec(tm, tn):
    return pl.BlockSpec((tm, tn), lambda j, i: (jnp.where(j > 0, i, 0), jnp.maximum(j - 1, 0)))


PROJ_TN = 512
PROJ_U_TILES = POOL_WIDTH // PROJ_TN
PROJ_Q_TILES = Q_WIDTH // PROJ_TN
PROJ_KV_TILES = 2 * KV_WIDTH // PROJ_TN


def _proj_kernel(a_ref, w_ref, u_ref, q_ref, kv_ref, scr):
    j, i, slot = _ws_ids()
    q_start = 1 + PROJ_U_TILES
    kv_start = q_start + PROJ_Q_TILES

    @pl.when(j == 0)
    def _():
        _cast_chunk(w_ref, scr, slot, i)

    @pl.when((j >= 1) & (j < q_start))
    def _():
        u_ref[...] = _dot(a_ref[...], scr[1 - slot])
        _cast_chunk(w_ref, scr, slot, i)

    @pl.when((j >= q_start) & (j < kv_start))
    def _():
        q_ref[...] = (_dot(a_ref[...], scr[1 - slot]) * HEAD_DIM ** -0.5).astype(q_ref.dtype)
        _cast_chunk(w_ref, scr, slot, i)

    @pl.when(j >= kv_start)
    def _():
        kv_ref[...] = _dot(a_ref[...], scr[1 - slot]).astype(kv_ref.dtype)
        _cast_chunk(w_ref, scr, slot, i)


def _proj_out_spec(tm, ni, first, n_tiles):
    def index(j, i):
        row = jnp.where(j < first, 0, jnp.where(j < first + n_tiles, i, ni - 1))
        return row, jnp.clip(j - first, 0, n_tiles - 1)
    return pl.BlockSpec((tm, PROJ_TN), index)


def _proj(a, w, layer, tm=1024):
    m, k = a.shape
    ni, nj = m // tm, IN_WIDTH // PROJ_TN
    q_start = 1 + PROJ_U_TILES
    return pl.pallas_call(
        _proj_kernel,
        grid=(nj + 1, ni),
        in_specs=[_a_spec(tm, k), _w_spec(layer, k // ni, PROJ_TN, nj)],
        out_specs=[_proj_out_spec(tm, ni, 1, PROJ_U_TILES),
                   _proj_out_spec(tm, ni, q_start, PROJ_Q_TILES),
                   _proj_out_spec(tm, ni, q_start + PROJ_Q_TILES, PROJ_KV_TILES)],
        out_shape=[jax.ShapeDtypeStruct((m, POOL_WIDTH), F32),
                   jax.ShapeDtypeStruct((m, Q_WIDTH), BF16),
                   jax.ShapeDtypeStruct((m, 2 * KV_WIDTH), BF16)],
        scratch_shapes=[pltpu.VMEM((2, k, PROJ_TN), BF16)],
        compiler_params=_params("arbitrary", "arbitrary"),
        name="proj",
    )(a, w)


POOL_PAD = 16


def _pool_kernel(u_ref, w_ref, s_ref, o_ref, a_ref, b_ref):
    g = pl.program_id(1)
    zeros = jnp.zeros((POOL_PAD, POOL_GROUP), F32)
    a_ref[0:POOL_PAD, :] = zeros
    b_ref[0:POOL_PAD, :] = zeros
    body = pl.ds(POOL_PAD, SEQ)
    a_ref[body, :] = u_ref[...]
    row = lax.broadcasted_iota(jnp.int32, (SEQ, 1), 0)

    def lagged(ref, lag):
        return ref[pl.ds(POOL_PAD - lag, SEQ), :]

    for group in range(N_GROUPS):
        @pl.when(g == group)
        def _():
            src, dst = a_ref, b_ref
            for level in range(group):
                dst[body, :] = src[body, :] + lagged(src, 1 << level)
                src, dst = dst, src
            total = src[body, :] + lagged(src, 1 << group)
            inv_count = 1.0 / jnp.minimum(row + 1, POOL_WINDOWS[group]).astype(F32)
            pooled = total * inv_count - u_ref[...]
            y = _dot(pooled.astype(BF16), w_ref[0, 0].astype(BF16))
            o_ref[...] = (y * s_ref[0]).astype(o_ref.dtype)


def _pool_branch(u, w_mix, scale, layer):
    return pl.pallas_call(
        _pool_kernel,
        grid=(BATCH, N_GROUPS),
        in_specs=[
            pl.BlockSpec((SEQ, POOL_GROUP), lambda b, g: (b, g)),
            pl.BlockSpec((1, 1, POOL_GROUP, POOL_GROUP), lambda b, g: (layer, g, 0, 0)),
            pl.BlockSpec((1, 1, POOL_GROUP), lambda b, g: (layer, 0, g)),
        ],
        out_specs=pl.BlockSpec((SEQ, POOL_GROUP), lambda b, g: (b, g)),
        out_shape=jax.ShapeDtypeStruct((ROWS, POOL_WIDTH), BF16),
        scratch_shapes=[pltpu.VMEM((POOL_PAD + SEQ, POOL_GROUP), F32),
                        pltpu.VMEM((POOL_PAD + SEQ, POOL_GROUP), F32)],
        compiler_params=_params("arbitrary", "arbitrary"),
        name="pool_branch",
    )(u, w_mix, scale.reshape(DEPTH, 1, POOL_WIDTH))


BAND = 2 * BLOCK
PAIRS = Q_PER_KV // 2


def _bias_kernel(rel_ref, bucket_ref, o_ref):
    h = pl.program_id(1)
    bucket = bucket_ref[0]
    for p in range(PAIRS):
        for e in range(2):
            head = h * Q_PER_KV + 2 * p + e
            acc = jnp.full(bucket.shape, NEG, F32)
            for b in range(N_BUCKETS):
                acc = jnp.where(bucket == b, rel_ref[b, head], acc)
            r0 = ((p // 2) * 2 + e) * BAND
            c0 = (p % 2) * BLOCK
            o_ref[0, 0, r0:r0 + BAND, c0:c0 + BLOCK] = acc


def _band_bias(rel_bias, bucket_pair):
    return pl.pallas_call(
        _bias_kernel,
        grid=(2, N_KV_HEADS),
        in_specs=[
            pl.BlockSpec(memory_space=pltpu.SMEM),
            pl.BlockSpec((1, BAND, BLOCK), lambda f, h: (f, 0, 0)),
        ],
        out_specs=pl.BlockSpec((1, 1, PAIRS * BAND, 2 * BLOCK), lambda f, h: (f, h, 0, 0)),
        out_shape=jax.ShapeDtypeStruct((2, N_KV_HEADS, PAIRS * BAND, 2 * BLOCK), F32),
        compiler_params=_params("arbitrary", "arbitrary"),
        name="band_bias",
    )(rel_bias, bucket_pair)


def _bucket_tables():
    i = jnp.arange(BLOCK)[None, :]
    j = jnp.arange(BAND)[:, None]
    dist = i + BLOCK - j
    max_exact = N_BUCKETS // 2
    d = jnp.maximum(dist, 0)
    log_ratio = jnp.log(jnp.maximum(d, 1).astype(F32) / max_exact) / math.log(MAX_DISTANCE / max_exact)
    large = jnp.minimum(max_exact + (log_ratio * (N_BUCKETS - max_exact)).astype(jnp.int32), N_BUCKETS - 1)
    bucket = jnp.where(d < max_exact, d, large)
    in_window = (dist >= 0) & (dist < WINDOW)
    later = jnp.where(in_window, bucket, -1)
    first = jnp.where(j >= BLOCK, later, -1)
    return jnp.stack([first, later]).astype(jnp.int32)


N_ATTN_STEPS = ROWS // BLOCK
ADA_SIDE_ROWS = D_MODEL // N_ATTN_STEPS
N_ATTN_GROUPS = N_KV_HEADS * (PAIRS // 2)
ADA_SIDE_COLS = N_ADA // N_ATTN_GROUPS


def _attn_kernel(sink_ref, q_ref, kvp_ref, kvc_ref, bias_ref, *rest, layer, with_mod):
    if with_mod:
        c_ref, wada_ref, bada_ref, o_ref, mod_ref = rest

        @pl.when(pl.program_id(0) == 0)
        def _():
            mod_ref[...] = jnp.broadcast_to(bada_ref[0], mod_ref.shape)

        cond = _silu_bf16(c_ref[0])

        def next_mod(n):
            cols = slice(n * ADA_SIDE_COLS, (n + 1) * ADA_SIDE_COLS)
            mod_ref[:, cols] += _dot(cond, wada_ref[0, :, cols].astype(BF16))
    else:
        o_ref, = rest

        def next_mod(n):
            pass

    lane = lax.broadcasted_iota(jnp.int32, (1, 2 * HEAD_DIM), 1)
    low = lane < HEAD_DIM
    left = lax.broadcasted_iota(jnp.int32, (1, 2 * BLOCK), 1) < BLOCK
    ones = jnp.ones((16, BAND), BF16)
    nt = (((1,), (1,)), ((), ()))
    keys, values = [], []
    for slab in range(N_KV_HEADS // 2):
        c = slab * 2 * HEAD_DIM
        k_slab = jnp.concatenate([kvp_ref[:, c:c + 2 * HEAD_DIM], kvc_ref[:, c:c + 2 * HEAD_DIM]], axis=0)
        v_slab = jnp.concatenate([kvp_ref[:, KV_WIDTH + c:KV_WIDTH + c + 2 * HEAD_DIM],
                                  kvc_ref[:, KV_WIDTH + c:KV_WIDTH + c + 2 * HEAD_DIM]], axis=0)
        k_swap = pltpu.roll(k_slab, HEAD_DIM, axis=1)
        v_t = v_slab.astype(F32).T.astype(BF16)
        zero = jnp.zeros_like(k_slab)
        for hh in range(2):
            k_on_low, k_on_high = (k_slab, k_swap) if hh == 0 else (k_swap, k_slab)
            keys.append((jnp.where(low, k_on_low, zero), jnp.where(low, zero, k_on_high)))
            values.append(jnp.concatenate([v_t[hh * HEAD_DIM:(hh + 1) * HEAD_DIM], ones], axis=0))

    def scores(h, t):
        c0 = h * Q_PER_KV * HEAD_DIM + t * 2 * BLOCK
        q2 = jnp.concatenate([q_ref[:, c0:c0 + BLOCK], q_ref[:, c0 + BLOCK:c0 + 2 * BLOCK]], axis=0)
        return [lax.dot_general(keys[h][e], q2, nt, preferred_element_type=F32)
                + bias_ref[0, h, (t * 2 + e) * BAND:(t * 2 + e + 1) * BAND, :] for e in range(2)]

    def softmax(h, t, s_pair):
        probs, tails = [], []
        for e in range(2):
            head0 = h * Q_PER_KV + 4 * t + e
            sink = jnp.where(left, sink_ref[layer, head0], sink_ref[layer, head0 + 2])
            m = jnp.maximum(jnp.max(s_pair[e], axis=0, keepdims=True), sink)
            probs.append(jnp.exp(s_pair[e] - m).astype(BF16))
            tails.append(jnp.exp(sink - m))
        return probs, tails

    def finish(h, t, probs, tails):
        for pp in range(2):
            cols = slice(pp * BLOCK, (pp + 1) * BLOCK)
            p_t = jnp.concatenate([probs[0][:, cols], probs[1][:, cols]], axis=1)
            tail = jnp.concatenate([tails[0][:, cols], tails[1][:, cols]], axis=1)
            res = _dot(values[h], p_t)
            out_t = res[:HEAD_DIM] * (1.0 / (res[HEAD_DIM:HEAD_DIM + 1] + tail))
            pair_t = jnp.concatenate([out_t[:, :BLOCK], out_t[:, BLOCK:]], axis=0)
            o0 = h * Q_PER_KV * HEAD_DIM + (2 * t + pp) * BLOCK
            o_ref[:, o0:o0 + BLOCK] = pair_t.T.astype(o_ref.dtype)

    groups = [(h, t) for h in range(N_KV_HEADS) for t in range(PAIRS // 2)]
    pending = scores(*groups[0])
    ready = None
    for n, group in enumerate(groups):
        current = pending
        if n + 1 < len(groups):
            pending = scores(*groups[n + 1])
        weights = softmax(*group, current)
        next_mod(n)
        if ready is not None:
            finish(*groups[n - 1], *ready)
        ready = weights
    finish(*groups[-1], *ready)


def _attention(q, kv, sinks, bias, layer, cond_slabs=None, w_ada=None, b_ada=None):
    with_mod = cond_slabs is not None
    in_specs = [
        pl.BlockSpec(memory_space=pltpu.SMEM),
        pl.BlockSpec((BLOCK, Q_WIDTH), lambda r: (r, 0)),
        pl.BlockSpec((BLOCK, 2 * KV_WIDTH), lambda r: (jnp.maximum(r - 1, 0), 0)),
        pl.BlockSpec((BLOCK, 2 * KV_WIDTH), lambda r: (r, 0)),
        pl.BlockSpec((1, N_KV_HEADS, PAIRS * BAND, 2 * BLOCK),
                     lambda r: (jnp.minimum(r % N_BLOCKS, 1), 0, 0, 0)),
    ]
    out_specs = [pl.BlockSpec((BLOCK, Q_WIDTH), lambda r: (r, 0))]
    out_shape = [jax.ShapeDtypeStruct((ROWS, Q_WIDTH), BF16)]
    args = [sinks, q, kv, kv, bias]
    if with_mod:
        in_specs += [
            pl.BlockSpec((1, 8, ADA_SIDE_ROWS), lambda r: (r, 0, 0)),
            pl.BlockSpec((1, ADA_SIDE_ROWS, N_ADA), lambda r: (layer + 1, r, 0)),
            pl.BlockSpec((1, 1, N_ADA), lambda r: (layer + 1, 0, 0)),
        ]
        out_specs.append(pl.BlockSpec((8, N_ADA), lambda r: (0, 0)))
        out_shape.append(jax.ShapeDtypeStruct((8, N_ADA), F32))
        args += [cond_slabs, w_ada, b_ada.reshape(DEPTH, 1, N_ADA)]
    return pl.pallas_call(
        functools.partial(_attn_kernel, layer=layer, with_mod=with_mod),
        grid=(N_ATTN_STEPS,),
        in_specs=in_specs,
        out_specs=out_specs,
        out_shape=out_shape,
        compiler_params=_params("arbitrary"),
        name="swa_attention",
    )(*args)


def _merge_kernel(h_ref, yp_ref, ya_ref, wgp_ref, wga_ref, wbp_ref, wba_ref, o_ref,
                  sgp, sga, sbp, sba):
    j, i, slot = _ws_ids()

    def cast():
        _cast_chunk(wgp_ref, sgp, slot, i)
        _cast_chunk(wga_ref, sga, slot, i)
        _cast_chunk(wbp_ref, sbp, slot, i)
        _cast_chunk(wba_ref, sba, slot, i)

    @pl.when(j == 0)
    def _():
        cast()

    @pl.when(j > 0)
    def _():
        h = h_ref[...]
        gate_pool = jax.nn.sigmoid(_dot(h, sgp[1 - slot]))
        gate_attn = jax.nn.sigmoid(_dot(h, sga[1 - slot]))
        y_pool = _dot(yp_ref[...], sbp[1 - slot])
        y_attn = _dot(ya_ref[...], sba[1 - slot])
        o_ref[...] = (gate_pool * y_pool + gate_attn * y_attn).astype(o_ref.dtype)
        cast()


def _merge(h, yp, ya, w_gate, w_bp, w_ba, layer, tm=512, tn=512):
    ni, nj = ROWS // tm, D_MODEL // tn
    return pl.pallas_call(
        _merge_kernel,
        grid=(nj + 1, ni),
        in_specs=[
            _a_spec(tm, D_MODEL), _a_spec(tm, POOL_WIDTH), _a_spec(tm, Q_WIDTH),
            _w_spec(layer, D_MODEL // ni, tn, nj),
            _w_spec(layer, D_MODEL // ni, tn, nj, col_tile0=nj),
            _w_spec(layer, POOL_WIDTH // ni, tn, nj),
            _w_spec(layer, Q_WIDTH // ni, tn, nj),
        ],
        out_specs=_o_spec(tm, tn),
        out_shape=jax.ShapeDtypeStruct((ROWS, D_MODEL), BF16),
        scratch_shapes=[pltpu.VMEM((2, D_MODEL, tn), BF16), pltpu.VMEM((2, D_MODEL, tn), BF16),
                        pltpu.VMEM((2, POOL_WIDTH, tn), BF16), pltpu.VMEM((2, Q_WIDTH, tn), BF16)],
        compiler_params=_params("arbitrary", "arbitrary"),
        name="gated_merge",
    )(h, yp, ya, w_gate, w_gate, w_bp, w_ba)


def _resid_kernel(a_ref, w_ref, x_ref, g_ref, o_ref, scr):
    j, i, slot = _ws_ids()

    @pl.when(j == 0)
    def _():
        _cast_chunk(w_ref, scr, slot, i)

    @pl.when(j > 0)
    def _():
        o_ref[...] = x_ref[...] + g_ref[0] * _dot(a_ref[...], scr[1 - slot])
        _cast_chunk(w_ref, scr, slot, i)


def _resid_proj(a, w, x, gate, layer, tm, tn):
    m, k = a.shape
    n = x.shape[1]
    ni, nj = m // tm, n // tn
    per_batch = SEQ // tm
    return pl.pallas_call(
        _resid_kernel,
        grid=(nj + 1, ni),
        in_specs=[
            _a_spec(tm, k),
            _w_spec(layer, k // ni, tn, nj),
            _o_spec(tm, tn),
            pl.BlockSpec((1, 1, tn), lambda j, i: (i // per_batch, 0, jnp.maximum(j - 1, 0))),
        ],
        out_specs=_o_spec(tm, tn),
        out_shape=jax.ShapeDtypeStruct((m, n), F32),
        scratch_shapes=[pltpu.VMEM((2, k, tn), BF16)],
        compiler_params=_params("arbitrary", "arbitrary"),
        name="resid_proj",
    )(a, w, x, gate)


FF_TILE = V7X_MXU_DIM
SWIGLU_ROWS = 1024


def _swiglu_kernel(a_ref, wa_ref, wb_ref, o_ref, scr):
    j, i, slot = _ws_ids()

    def cast():
        _cast_chunk(wa_ref, scr, slot, i)
        _cast_chunk(wb_ref, scr, slot, i, col0=FF_TILE)

    @pl.when(j == 0)
    def _():
        cast()

    @pl.when(j > 0)
    def _():
        w = scr[1 - slot]
        for r0 in range(0, a_ref.shape[0], SWIGLU_ROWS):
            ab = _dot(a_ref[r0:r0 + SWIGLU_ROWS, :], w)
            a = ab[:, :FF_TILE]
            b = ab[:, FF_TILE:]
            o_ref[r0:r0 + SWIGLU_ROWS, :] = ((a * jax.nn.sigmoid(a)) * b).astype(o_ref.dtype)
        cast()


def _swiglu(h2, w_ffn_in, layer, tm=2048):
    ni, nj = ROWS // tm, D_FF // FF_TILE
    return pl.pallas_call(
        _swiglu_kernel,
        grid=(nj + 1, ni),
        in_specs=[
            _a_spec(tm, D_MODEL),
            _w_spec(layer, D_MODEL // ni, FF_TILE, nj),
            _w_spec(layer, D_MODEL // ni, FF_TILE, nj, col_tile0=nj),
        ],
        out_specs=_o_spec(tm, FF_TILE),
        out_shape=jax.ShapeDtypeStruct((ROWS, D_FF), BF16),
        scratch_shapes=[pltpu.VMEM((2, D_MODEL, 2 * FF_TILE), BF16)],
        compiler_params=_params("arbitrary", "arbitrary"),
        name="swiglu_up",
    )(h2, w_ffn_in, w_ffn_in)


def kernel(x, c, w_ada, b_ada, norm1, w_in, w_pool_mix, pool_scale, sinks, rel_bias,
           w_branch_pool, w_branch_attn, w_gate, w_out, norm2, w_ffn_in, w_ffn_out,
           final_norm):
    c_pad = jnp.pad(c, ((0, 8 - BATCH), (0, 0)))
    cond_slabs = c_pad.reshape(8, N_ATTN_STEPS, ADA_SIDE_ROWS).transpose(1, 0, 2)
    mod = _ada_first(c_pad, w_ada, b_ada)
    bias = _band_bias(rel_bias, _bucket_tables())
    xr = x.reshape(ROWS, D_MODEL)

    for l in range(DEPTH):
        sh1, sc1, g1, sh2, sc2, g2 = [m.reshape(BATCH, 1, D_MODEL)
                                      for m in jnp.split(mod[:BATCH], N_MOD, axis=-1)]
        h = _mod_norm(xr, norm1, l, sc1, sh1)
        u, q, kv = _proj(h, w_in, l)
        yp = _pool_branch(u, w_pool_mix, pool_scale, l)
        if l + 1 < DEPTH:
            ya, mod = _attention(q, kv, sinks, bias, l, cond_slabs, w_ada, b_ada)
        else:
            ya, = _attention(q, kv, sinks, bias, l)
        merged = _merge(h, yp, ya, w_gate, w_branch_pool, w_branch_attn, l)
        xr = _resid_proj(merged, w_out, xr, g1, l, tm=512, tn=1024)

        h2 = _mod_norm(xr, norm2, l, sc2, sh2)
        act = _swiglu(h2, w_ffn_in, l)
        xr = _resid_proj(act, w_ffn_out, xr, g2, l, tm=512, tn=512)

    return _final_norm(xr, final_norm).reshape(BATCH, SEQ, D_MODEL)
```

```python
import functools
import math

import jax
import jax.numpy as jnp
from jax import lax
from jax.experimental import pallas as pl
from jax.experimental.pallas import tpu as pltpu

D_MODEL = 4096
BATCH = 4
SEQ = 2048
ROWS = BATCH * SEQ
DEPTH = 2
POOL_WINDOWS = (2, 4, 8, 16)
N_GROUPS = len(POOL_WINDOWS)
POOL_WIDTH = D_MODEL // 2
POOL_GROUP = POOL_WIDTH // N_GROUPS
HEAD_DIM = 64
N_Q_HEADS = (D_MODEL // 2) // HEAD_DIM
N_KV_HEADS = N_Q_HEADS // 8
Q_PER_KV = N_Q_HEADS // N_KV_HEADS
Q_WIDTH = N_Q_HEADS * HEAD_DIM
KV_WIDTH = N_KV_HEADS * HEAD_DIM
WINDOW = 128
BLOCK = 128
N_BLOCKS = SEQ // BLOCK
N_BUCKETS = 32
MAX_DISTANCE = 128
IN_WIDTH = POOL_WIDTH + Q_WIDTH + 2 * KV_WIDTH
D_FF = -(-(8 * D_MODEL) // (3 * 256)) * 256
N_MOD = 6
EPS = 1e-6

V7X_VMEM_BYTES = 64 * 1024 * 1024
VMEM_LIMIT = V7X_VMEM_BYTES - 8 * 1024 * 1024
V7X_MXU_DIM = 256
V7X_SUBLANES = 8
V7X_BF16_SUBLANES = 16
COND_ROWS = V7X_SUBLANES

NEG = -1e30

BF16 = jnp.bfloat16
F32 = jnp.float32


def _params(*sem):
    return pltpu.CompilerParams(dimension_semantics=sem, vmem_limit_bytes=VMEM_LIMIT)


def _dot(a, b):
    return jnp.dot(a, b, preferred_element_type=F32)


ADA_ROWS = 128
N_ADA = N_MOD * D_MODEL


def _silu_bf16(c):
    return (c * jax.nn.sigmoid(c)).astype(BF16)


def _ada_kernel(c_ref, w_ref, b_ref, o_ref):
    k = pl.program_id(0)

    @pl.when(k == 0)
    def _():
        o_ref[...] = jnp.broadcast_to(b_ref[0], o_ref.shape)

    o_ref[...] += _dot(_silu_bf16(c_ref[...]), w_ref[0].astype(BF16))


def _ada_first(c_pad, w_ada, b_ada):
    return pl.pallas_call(
        _ada_kernel,
        grid=(D_MODEL // ADA_ROWS,),
        in_specs=[
            pl.BlockSpec((COND_ROWS, ADA_ROWS), lambda k: (0, k)),
            pl.BlockSpec((1, ADA_ROWS, N_ADA), lambda k: (0, k, 0)),
            pl.BlockSpec((1, 1, N_ADA), lambda k: (0, 0, 0)),
        ],
        out_specs=pl.BlockSpec((COND_ROWS, N_ADA), lambda k: (0, 0)),
        out_shape=jax.ShapeDtypeStruct((COND_ROWS, N_ADA), F32),
        compiler_params=_params("arbitrary"),
        name="ada_mod",
    )(c_pad, w_ada, b_ada.reshape(DEPTH, 1, N_ADA))


NORM_CHUNK = 64


def _norm_kernel(x_ref, g_ref, sc_ref, sh_ref, o_ref):
    gain = g_ref[0] * (1.0 + sc_ref[0])
    shift = sh_ref[0]

    def chunk(c, carry):
        rows = pl.ds(pl.multiple_of(c * NORM_CHUNK, NORM_CHUNK), NORM_CHUNK)
        x = x_ref[rows, :]
        r = lax.rsqrt(jnp.mean(x * x, axis=-1, keepdims=True) + EPS)
        o_ref[rows, :] = ((x * r) * gain + shift).astype(o_ref.dtype)
        return carry

    lax.fori_loop(0, x_ref.shape[0] // NORM_CHUNK, chunk, 0)


def _mod_norm(x, gain, layer, scale, shift):
    tm = 1024
    per_batch = SEQ // tm
    row = lambda i: (i // per_batch, 0, 0)
    return pl.pallas_call(
        _norm_kernel,
        grid=(ROWS // tm,),
        in_specs=[
            pl.BlockSpec((tm, D_MODEL), lambda i: (i, 0)),
            pl.BlockSpec((1, 1, D_MODEL), lambda i: (layer, 0, 0)),
            pl.BlockSpec((1, 1, D_MODEL), row),
            pl.BlockSpec((1, 1, D_MODEL), row),
        ],
        out_specs=pl.BlockSpec((tm, D_MODEL), lambda i: (i, 0)),
        out_shape=jax.ShapeDtypeStruct((ROWS, D_MODEL), BF16),
        compiler_params=_params("arbitrary"),
        name="mod_norm",
    )(x, gain.reshape(DEPTH, 1, D_MODEL), scale, shift)


def _final_norm_kernel(x_ref, g_ref, o_ref):
    gain = g_ref[...]

    def chunk(c, carry):
        rows = pl.ds(pl.multiple_of(c * NORM_CHUNK, NORM_CHUNK), NORM_CHUNK)
        x = x_ref[rows, :]
        r = lax.rsqrt(jnp.mean(x * x, axis=-1, keepdims=True) + EPS)
        o_ref[rows, :] = x * r * gain
        return carry

    lax.fori_loop(0, x_ref.shape[0] // NORM_CHUNK, chunk, 0)


def _final_norm(x, gain):
    tm = 512
    return pl.pallas_call(
        _final_norm_kernel,
        grid=(ROWS // tm,),
        in_specs=[
            pl.BlockSpec((tm, D_MODEL), lambda i: (i, 0)),
            pl.BlockSpec((1, D_MODEL), lambda i: (0, 0)),
        ],
        out_specs=pl.BlockSpec((tm, D_MODEL), lambda i: (i, 0)),
        out_shape=jax.ShapeDtypeStruct((ROWS, D_MODEL), F32),
        compiler_params=_params("arbitrary"),
        name="final_norm",
    )(x, gain.reshape(1, D_MODEL))


def _cast_chunk(w_ref, scr, slot, step, col0=0):
    rows, cols = w_ref.shape[1], w_ref.shape[2]
    r0 = pl.multiple_of(step * rows, rows)
    scr[slot, pl.ds(r0, rows), col0:col0 + cols] = w_ref[0].astype(BF16)


def _ws_ids():
    j = pl.program_id(0)
    return j, pl.program_id(1), j % 2


def _w_spec(layer, k, ni, cols, n_tiles, col_tile0=0):
    return pl.BlockSpec((1, k // ni, cols),
                        lambda j, i: (layer, i, col_tile0 + jnp.minimum(j, n_tiles - 1)))


def _a_spec(tm, k, col_block=0):
    return pl.BlockSpec((tm, k), lambda j, i: (jnp.where(j > 0, i, 0), col_block))


def _o_spec(tm, tn):
    return pl.BlockSpec((tm, tn), lambda j, i: (jnp.where(j > 0, i, 0), jnp.maximum(j - 1, 0)))


PROJ_TN = 512
PROJ_U_TILES = POOL_WIDTH // PROJ_TN
PROJ_Q_TILES = Q_WIDTH // PROJ_TN
PROJ_KV_TILES = 2 * KV_WIDTH // PROJ_TN


def _proj_kernel(a_ref, w_ref, u_ref, q_ref, kv_ref, scr):
    j, i, slot = _ws_ids()
    q_start = 1 + PROJ_U_TILES
    kv_start = q_start + PROJ_Q_TILES

    @pl.when(j == 0)
    def _():
        _cast_chunk(w_ref, scr, slot, i)

    @pl.when((j >= 1) & (j < q_start))
    def _():
        u_ref[...] = _dot(a_ref[...], scr[1 - slot])
        _cast_chunk(w_ref, scr, slot, i)

    @pl.when((j >= q_start) & (j < kv_start))
    def _():
        q_ref[...] = (_dot(a_ref[...], scr[1 - slot]) * HEAD_DIM ** -0.5).astype(q_ref.dtype)
        _cast_chunk(w_ref, scr, slot, i)

    @pl.when(j >= kv_start)
    def _():
        kv_ref[...] = _dot(a_ref[...], scr[1 - slot]).astype(kv_ref.dtype)
        _cast_chunk(w_ref, scr, slot, i)


def _proj_out_spec(tm, ni, first, n_tiles):
    def index(j, i):
        row = jnp.where(j < first, 0, jnp.where(j < first + n_tiles, i, ni - 1))
        return row, jnp.clip(j - first, 0, n_tiles - 1)
    return pl.BlockSpec((tm, PROJ_TN), index)


def _proj(a, w, layer, tm=1024):
    m, k = a.shape
    ni, nj = m // tm, IN_WIDTH // PROJ_TN
    q_start = 1 + PROJ_U_TILES
    return pl.pallas_call(
        _proj_kernel,
        grid=(nj + 1, ni),
        in_specs=[_a_spec(tm, k), _w_spec(layer, k, ni, PROJ_TN, nj)],
        out_specs=[_proj_out_spec(tm, ni, 1, PROJ_U_TILES),
                   _proj_out_spec(tm, ni, q_start, PROJ_Q_TILES),
                   _proj_out_spec(tm, ni, q_start + PROJ_Q_TILES, PROJ_KV_TILES)],
        out_shape=[jax.ShapeDtypeStruct((m, POOL_WIDTH), F32),
                   jax.ShapeDtypeStruct((m, Q_WIDTH), BF16),
                   jax.ShapeDtypeStruct((m, 2 * KV_WIDTH), BF16)],
        scratch_shapes=[pltpu.VMEM((2, k, PROJ_TN), BF16)],
        compiler_params=_params("arbitrary", "arbitrary"),
        name="proj",
    )(a, w)


POOL_PAD = 16


def _pool_kernel(u_ref, w_ref, s_ref, o_ref, a_ref, b_ref):
    g = pl.program_id(1)
    zeros = jnp.zeros((POOL_PAD, POOL_GROUP), F32)
    a_ref[0:POOL_PAD, :] = zeros
    b_ref[0:POOL_PAD, :] = zeros
    body = pl.ds(POOL_PAD, SEQ)
    a_ref[body, :] = u_ref[...]
    row = lax.broadcasted_iota(jnp.int32, (SEQ, 1), 0)

    def lagged(ref, lag):
        return ref[pl.ds(POOL_PAD - lag, SEQ), :]

    for group in range(N_GROUPS):
        @pl.when(g == group)
        def _():
            src, dst = a_ref, b_ref
            for level in range(group):
                dst[body, :] = src[body, :] + lagged(src, 1 << level)
                src, dst = dst, src
            total = src[body, :] + lagged(src, 1 << group)
            inv_count = 1.0 / jnp.minimum(row + 1, POOL_WINDOWS[group]).astype(F32)
            pooled = total * inv_count - u_ref[...]
            y = _dot(pooled.astype(BF16), w_ref[0, 0].astype(BF16))
            o_ref[...] = (y * s_ref[0]).astype(o_ref.dtype)


def _pool_branch(u, w_mix, scale, layer):
    return pl.pallas_call(
        _pool_kernel,
        grid=(BATCH, N_GROUPS),
        in_specs=[
            pl.BlockSpec((SEQ, POOL_GROUP), lambda b, g: (b, g)),
            pl.BlockSpec((1, 1, POOL_GROUP, POOL_GROUP), lambda b, g: (layer, g, 0, 0)),
            pl.BlockSpec((1, 1, POOL_GROUP), lambda b, g: (layer, 0, g)),
        ],
        out_specs=pl.BlockSpec((SEQ, POOL_GROUP), lambda b, g: (b, g)),
        out_shape=jax.ShapeDtypeStruct((ROWS, POOL_WIDTH), BF16),
        scratch_shapes=[pltpu.VMEM((POOL_PAD + SEQ, POOL_GROUP), F32),
                        pltpu.VMEM((POOL_PAD + SEQ, POOL_GROUP), F32)],
        compiler_params=_params("arbitrary", "arbitrary"),
        name="pool_branch",
    )(u, w_mix, scale.reshape(DEPTH, 1, POOL_WIDTH))


BAND = 2 * BLOCK
PAIRS = Q_PER_KV // 2


def _bias_kernel(rel_ref, bucket_ref, o_ref):
    h = pl.program_id(1)
    bucket = bucket_ref[0]
    for p in range(PAIRS):
        for e in range(2):
            head = h * Q_PER_KV + 2 * p + e
            acc = jnp.full(bucket.shape, NEG, F32)
            for b in range(N_BUCKETS):
                acc = jnp.where(bucket == b, rel_ref[b, head], acc)
            r0 = ((p // 2) * 2 + e) * BAND
            c0 = (p % 2) * BLOCK
            o_ref[0, 0, r0:r0 + BAND, c0:c0 + BLOCK] = acc


def _band_bias(rel_bias, bucket_pair):
    return pl.pallas_call(
        _bias_kernel,
        grid=(2, N_KV_HEADS),
        in_specs=[
            pl.BlockSpec(memory_space=pltpu.SMEM),
            pl.BlockSpec((1, BAND, BLOCK), lambda f, h: (f, 0, 0)),
        ],
        out_specs=pl.BlockSpec((1, 1, PAIRS * BAND, 2 * BLOCK), lambda f, h: (f, h, 0, 0)),
        out_shape=jax.ShapeDtypeStruct((2, N_KV_HEADS, PAIRS * BAND, 2 * BLOCK), F32),
        compiler_params=_params("arbitrary", "arbitrary"),
        name="band_bias",
    )(rel_bias, bucket_pair)


def _bucket_tables():
    i = jnp.arange(BLOCK)[None, :]
    j = jnp.arange(BAND)[:, None]
    dist = i + BLOCK - j
    max_exact = N_BUCKETS // 2
    d = jnp.maximum(dist, 0)
    log_ratio = jnp.log(jnp.maximum(d, 1).astype(F32) / max_exact) / math.log(MAX_DISTANCE / max_exact)
    large = jnp.minimum(max_exact + (log_ratio * (N_BUCKETS - max_exact)).astype(jnp.int32), N_BUCKETS - 1)
    bucket = jnp.where(d < max_exact, d, large)
    in_window = (dist >= 0) & (dist < WINDOW)
    later = jnp.where(in_window, bucket, -1)
    first = jnp.where(j >= BLOCK, later, -1)
    return jnp.stack([first, later]).astype(jnp.int32)


N_ATTN_STEPS = ROWS // BLOCK
ADA_SIDE_ROWS = D_MODEL // N_ATTN_STEPS
N_ATTN_GROUPS = N_KV_HEADS * (PAIRS // 2)
ADA_SIDE_COLS = N_ADA // N_ATTN_GROUPS


def _attn_kernel(sink_ref, q_ref, kvp_ref, kvc_ref, bias_ref, *rest, layer, with_mod):
    if with_mod:
        c_ref, wada_ref, bada_ref, o_ref, mod_ref = rest

        @pl.when(pl.program_id(0) == 0)
        def _():
            mod_ref[...] = jnp.broadcast_to(bada_ref[0], mod_ref.shape)

        cond = _silu_bf16(c_ref[0])

        def next_mod(n):
            cols = slice(n * ADA_SIDE_COLS, (n + 1) * ADA_SIDE_COLS)
            mod_ref[:, cols] += _dot(cond, wada_ref[0, :, cols].astype(BF16))
    else:
        o_ref, = rest

        def next_mod(n):
            pass

    lane = lax.broadcasted_iota(jnp.int32, (1, 2 * HEAD_DIM), 1)
    low = lane < HEAD_DIM
    left = lax.broadcasted_iota(jnp.int32, (1, 2 * BLOCK), 1) < BLOCK
    ones = jnp.ones((V7X_BF16_SUBLANES, BAND), BF16)
    nt = (((1,), (1,)), ((), ()))
    keys, values = [], []
    for slab in range(N_KV_HEADS // 2):
        c = slab * 2 * HEAD_DIM
        k_slab = jnp.concatenate([kvp_ref[:, c:c + 2 * HEAD_DIM], kvc_ref[:, c:c + 2 * HEAD_DIM]], axis=0)
        v_slab = jnp.concatenate([kvp_ref[:, KV_WIDTH + c:KV_WIDTH + c + 2 * HEAD_DIM],
                                  kvc_ref[:, KV_WIDTH + c:KV_WIDTH + c + 2 * HEAD_DIM]], axis=0)
        k_swap = pltpu.roll(k_slab, HEAD_DIM, axis=1)
        v_t = v_slab.astype(F32).T.astype(BF16)
        zero = jnp.zeros_like(k_slab)
        for hh in range(2):
            k_on_low, k_on_high = (k_slab, k_swap) if hh == 0 else (k_swap, k_slab)
            keys.append((jnp.where(low, k_on_low, zero), jnp.where(low, zero, k_on_high)))
            values.append(jnp.concatenate([v_t[hh * HEAD_DIM:(hh + 1) * HEAD_DIM], ones], axis=0))

    def scores(h, t):
        c0 = h * Q_PER_KV * HEAD_DIM + t * 2 * BLOCK
        q2 = jnp.concatenate([q_ref[:, c0:c0 + BLOCK], q_ref[:, c0 + BLOCK:c0 + 2 * BLOCK]], axis=0)
        return [lax.dot_general(keys[h][e], q2, nt, preferred_element_type=F32)
                + bias_ref[0, h, (t * 2 + e) * BAND:(t * 2 + e + 1) * BAND, :] for e in range(2)]

    def softmax(h, t, s_pair):
        probs, tails = [], []
        for e in range(2):
            head0 = h * Q_PER_KV + 4 * t + e
            sink = jnp.where(left, sink_ref[layer, head0], sink_ref[layer, head0 + 2])
            m = jnp.maximum(jnp.max(s_pair[e], axis=0, keepdims=True), sink)
            probs.append(jnp.exp(s_pair[e] - m).astype(BF16))
            tails.append(jnp.exp(sink - m))
        return probs, tails

    def finish(h, t, probs, tails):
        for pp in range(2):
            cols = slice(pp * BLOCK, (pp + 1) * BLOCK)
            p_t = jnp.concatenate([probs[0][:, cols], probs[1][:, cols]], axis=1)
            tail = jnp.concatenate([tails[0][:, cols], tails[1][:, cols]], axis=1)
            res = _dot(values[h], p_t)
            out_t = res[:HEAD_DIM] * (1.0 / (res[HEAD_DIM:HEAD_DIM + 1] + tail))
            pair_t = jnp.concatenate([out_t[:, :BLOCK], out_t[:, BLOCK:]], axis=0)
            o0 = h * Q_PER_KV * HEAD_DIM + (2 * t + pp) * BLOCK
            o_ref[:, o0:o0 + BLOCK] = pair_t.T.astype(o_ref.dtype)

    groups = [(h, t) for h in range(N_KV_HEADS) for t in range(PAIRS // 2)]
    pending = scores(*groups[0])
    ready = None
    for n, group in enumerate(groups):
        current = pending
        if n + 1 < len(groups):
            pending = scores(*groups[n + 1])
        weights = softmax(*group, current)
        next_mod(n)
        if ready is not None:
            finish(*groups[n - 1], *ready)
        ready = weights
    finish(*groups[-1], *ready)


def _attention(q, kv, sinks, bias, layer, cond_slabs=None, w_ada=None, b_ada=None):
    with_mod = cond_slabs is not None
    in_specs = [
        pl.BlockSpec(memory_space=pltpu.SMEM),
        pl.BlockSpec((BLOCK, Q_WIDTH), lambda r: (r, 0)),
        pl.BlockSpec((BLOCK, 2 * KV_WIDTH), lambda r: (jnp.maximum(r - 1, 0), 0)),
        pl.BlockSpec((BLOCK, 2 * KV_WIDTH), lambda r: (r, 0)),
        pl.BlockSpec((1, N_KV_HEADS, PAIRS * BAND, 2 * BLOCK),
                     lambda r: (jnp.minimum(r % N_BLOCKS, 1), 0, 0, 0)),
    ]
    out_specs = [pl.BlockSpec((BLOCK, Q_WIDTH), lambda r: (r, 0))]
    out_shape = [jax.ShapeDtypeStruct((ROWS, Q_WIDTH), BF16)]
    args = [sinks, q, kv, kv, bias]
    if with_mod:
        in_specs += [
            pl.BlockSpec((1, COND_ROWS, ADA_SIDE_ROWS), lambda r: (r, 0, 0)),
            pl.BlockSpec((1, ADA_SIDE_ROWS, N_ADA), lambda r: (layer + 1, r, 0)),
            pl.BlockSpec((1, 1, N_ADA), lambda r: (layer + 1, 0, 0)),
        ]
        out_specs.append(pl.BlockSpec((COND_ROWS, N_ADA), lambda r: (0, 0)))
        out_shape.append(jax.ShapeDtypeStruct((COND_ROWS, N_ADA), F32))
        args += [cond_slabs, w_ada, b_ada.reshape(DEPTH, 1, N_ADA)]
    return pl.pallas_call(
        functools.partial(_attn_kernel, layer=layer, with_mod=with_mod),
        grid=(N_ATTN_STEPS,),
        in_specs=in_specs,
        out_specs=out_specs,
        out_shape=out_shape,
        compiler_params=_params("arbitrary"),
        name="swa_attention",
    )(*args)


def _merge_kernel(h_ref, yp_ref, ya_ref, wgp_ref, wga_ref, wbp_ref, wba_ref, o_ref,
                  sgp, sga, sbp, sba):
    j, i, slot = _ws_ids()

    def cast():
        _cast_chunk(wgp_ref, sgp, slot, i)
        _cast_chunk(wga_ref, sga, slot, i)
        _cast_chunk(wbp_ref, sbp, slot, i)
        _cast_chunk(wba_ref, sba, slot, i)

    @pl.when(j == 0)
    def _():
        cast()

    @pl.when(j > 0)
    def _():
        h = h_ref[...]
        gate_pool = jax.nn.sigmoid(_dot(h, sgp[1 - slot]))
        gate_attn = jax.nn.sigmoid(_dot(h, sga[1 - slot]))
        y_pool = _dot(yp_ref[...], sbp[1 - slot])
        y_attn = _dot(ya_ref[...], sba[1 - slot])
        o_ref[...] = (gate_pool * y_pool + gate_attn * y_attn).astype(o_ref.dtype)
        cast()


def _merge(h, yp, ya, w_gate, w_bp, w_ba, layer, tm=512, tn=512):
    ni, nj = ROWS // tm, D_MODEL // tn
    return pl.pallas_call(
        _merge_kernel,
        grid=(nj + 1, ni),
        in_specs=[
            _a_spec(tm, D_MODEL), _a_spec(tm, POOL_WIDTH), _a_spec(tm, Q_WIDTH),
            _w_spec(layer, D_MODEL, ni, tn, nj),
            _w_spec(layer, D_MODEL, ni, tn, nj, col_tile0=nj),
            _w_spec(layer, POOL_WIDTH, ni, tn, nj),
            _w_spec(layer, Q_WIDTH, ni, tn, nj),
        ],
        out_specs=_o_spec(tm, tn),
        out_shape=jax.ShapeDtypeStruct((ROWS, D_MODEL), BF16),
        scratch_shapes=[pltpu.VMEM((2, D_MODEL, tn), BF16), pltpu.VMEM((2, D_MODEL, tn), BF16),
                        pltpu.VMEM((2, POOL_WIDTH, tn), BF16), pltpu.VMEM((2, Q_WIDTH, tn), BF16)],
        compiler_params=_params("arbitrary", "arbitrary"),
        name="gated_merge",
    )(h, yp, ya, w_gate, w_gate, w_bp, w_ba)


def _resid_kernel(a_ref, w_ref, x_ref, g_ref, o_ref, scr):
    j, i, slot = _ws_ids()

    @pl.when(j == 0)
    def _():
        _cast_chunk(w_ref, scr, slot, i)

    @pl.when(j > 0)
    def _():
        o_ref[...] = x_ref[...] + g_ref[0] * _dot(a_ref[...], scr[1 - slot])
        _cast_chunk(w_ref, scr, slot, i)


def _resid_proj(a, w, x, gate, layer, tm, tn):
    m, k = a.shape
    n = x.shape[1]
    ni, nj = m // tm, n // tn
    per_batch = SEQ // tm
    return pl.pallas_call(
        _resid_kernel,
        grid=(nj + 1, ni),
        in_specs=[
            _a_spec(tm, k),
            _w_spec(layer, k, ni, tn, nj),
            _o_spec(tm, tn),
            pl.BlockSpec((1, 1, tn), lambda j, i: (i // per_batch, 0, jnp.maximum(j - 1, 0))),
        ],
        out_specs=_o_spec(tm, tn),
        out_shape=jax.ShapeDtypeStruct((m, n), F32),
        scratch_shapes=[pltpu.VMEM((2, k, tn), BF16)],
        compiler_params=_params("arbitrary", "arbitrary"),
        name="resid_proj",
    )(a, w, x, gate)


FF_TILE = V7X_MXU_DIM
SWIGLU_ROWS = 1024


def _swiglu_kernel(a_ref, wa_ref, wb_ref, o_ref, scr):
    j, i, slot = _ws_ids()

    def cast():
        _cast_chunk(wa_ref, scr, slot, i)
        _cast_chunk(wb_ref, scr, slot, i, col0=FF_TILE)

    @pl.when(j == 0)
    def _():
        cast()

    @pl.when(j > 0)
    def _():
        w = scr[1 - slot]
        for r0 in range(0, a_ref.shape[0], SWIGLU_ROWS):
            ab = _dot(a_ref[r0:r0 + SWIGLU_ROWS, :], w)
            a = ab[:, :FF_TILE]
            b = ab[:, FF_TILE:]
            o_ref[r0:r0 + SWIGLU_ROWS, :] = ((a * jax.nn.sigmoid(a)) * b).astype(o_ref.dtype)
        cast()


def _swiglu(h2, w_ffn_in, layer, tm=2048):
    ni, nj = ROWS // tm, D_FF // FF_TILE
    return pl.pallas_call(
        _swiglu_kernel,
        grid=(nj + 1, ni),
        in_specs=[
            _a_spec(tm, D_MODEL),
            _w_spec(layer, D_MODEL, ni, FF_TILE, nj),
            _w_spec(layer, D_MODEL, ni, FF_TILE, nj, col_tile0=nj),
        ],
        out_specs=_o_spec(tm, FF_TILE),
        out_shape=jax.ShapeDtypeStruct((ROWS, D_FF), BF16),
        scratch_shapes=[pltpu.VMEM((2, D_MODEL, 2 * FF_TILE), BF16)],
        compiler_params=_params("arbitrary", "arbitrary"),
        name="swiglu_up",
    )(h2, w_ffn_in, w_ffn_in)


def kernel(x, c, w_ada, b_ada, norm1, w_in, w_pool_mix, pool_scale, sinks, rel_bias,
           w_branch_pool, w_branch_attn, w_gate, w_out, norm2, w_ffn_in, w_ffn_out,
           final_norm):
    c_pad = jnp.pad(c, ((0, COND_ROWS - BATCH), (0, 0)))
    cond_slabs = c_pad.reshape(COND_ROWS, N_ATTN_STEPS, ADA_SIDE_ROWS).transpose(1, 0, 2)
    mod = _ada_first(c_pad, w_ada, b_ada)
    bias = _band_bias(rel_bias, _bucket_tables())
    xr = x.reshape(ROWS, D_MODEL)

    for l in range(DEPTH):
        sh1, sc1, g1, sh2, sc2, g2 = [m.reshape(BATCH, 1, D_MODEL)
                                      for m in jnp.split(mod[:BATCH], N_MOD, axis=-1)]
        h = _mod_norm(xr, norm1, l, sc1, sh1)
        u, q, kv = _proj(h, w_in, l)
        yp = _pool_branch(u, w_pool_mix, pool_scale, l)
        if l + 1 < DEPTH:
            ya, mod = _attention(q, kv, sinks, bias, l, cond_slabs, w_ada, b_ada)
        else:
            ya, = _attention(q, kv, sinks, bias, l)
        merged = _merge(h, yp, ya, w_gate, w_branch_pool, w_branch_attn, l)
        xr = _resid_proj(merged, w_out, xr, g1, l, tm=512, tn=1024)

        h2 = _mod_norm(xr, norm2, l, sc2, sh2)
        act = _swiglu(h2, w_ffn_in, l)
        xr = _resid_proj(act, w_ffn_out, xr, g2, l, tm=512, tn=512)

    return _final_norm(xr, final_norm).reshape(BATCH, SEQ, D_MODEL)
```

```python
import functools
import math

import jax
import jax.numpy as jnp
from jax import lax
from jax.experimental import pallas as pl
from jax.experimental.pallas import tpu as pltpu

D_MODEL = 4096
BATCH = 4
SEQ = 2048
ROWS = BATCH * SEQ
DEPTH = 2
POOL_WINDOWS = (2, 4, 8, 16)
N_GROUPS = len(POOL_WINDOWS)
POOL_WIDTH = D_MODEL // 2
POOL_GROUP = POOL_WIDTH // N_GROUPS
HEAD_DIM = 64
N_Q_HEADS = (D_MODEL // 2) // HEAD_DIM
N_KV_HEADS = N_Q_HEADS // 8
Q_PER_KV = N_Q_HEADS // N_KV_HEADS
Q_WIDTH = N_Q_HEADS * HEAD_DIM
KV_WIDTH = N_KV_HEADS * HEAD_DIM
WINDOW = 128
BLOCK = 128
N_BLOCKS = SEQ // BLOCK
N_BUCKETS = 32
MAX_DISTANCE = 128
IN_WIDTH = POOL_WIDTH + Q_WIDTH + 2 * KV_WIDTH
D_FF = -(-(8 * D_MODEL) // (3 * 256)) * 256
N_MOD = 6
EPS = 1e-6

V7X_VMEM_BYTES = 64 * 1024 * 1024
VMEM_LIMIT = V7X_VMEM_BYTES - 8 * 1024 * 1024
V7X_MXU_DIM = 256
V7X_SUBLANES = 8
V7X_BF16_SUBLANES = 16
COND_ROWS = V7X_SUBLANES

NEG = -1e30

BF16 = jnp.bfloat16
F32 = jnp.float32


def _params(*sem):
    return pltpu.CompilerParams(dimension_semantics=sem, vmem_limit_bytes=VMEM_LIMIT)


def _dot(a, b):
    return jnp.dot(a, b, preferred_element_type=F32)


ADA_ROWS = 128
N_ADA = N_MOD * D_MODEL


def _silu_bf16(c):
    return (c * jax.nn.sigmoid(c)).astype(BF16)


def _ada_kernel(c_ref, w_ref, b_ref, o_ref):
    k = pl.program_id(0)

    @pl.when(k == 0)
    def _():
        o_ref[...] = jnp.broadcast_to(b_ref[0], o_ref.shape)

    o_ref[...] += _dot(_silu_bf16(c_ref[...]), w_ref[0].astype(BF16))


def _ada_first(c_pad, w_ada, b_ada):
    return pl.pallas_call(
        _ada_kernel,
        grid=(D_MODEL // ADA_ROWS,),
        in_specs=[
            pl.BlockSpec((COND_ROWS, ADA_ROWS), lambda k: (0, k)),
            pl.BlockSpec((1, ADA_ROWS, N_ADA), lambda k: (0, k, 0)),
            pl.BlockSpec((1, 1, N_ADA), lambda k: (0, 0, 0)),
        ],
        out_specs=pl.BlockSpec((COND_ROWS, N_ADA), lambda k: (0, 0)),
        out_shape=jax.ShapeDtypeStruct((COND_ROWS, N_ADA), F32),
        compiler_params=_params("arbitrary"),
        name="ada_mod",
    )(c_pad, w_ada, b_ada.reshape(DEPTH, 1, N_ADA))


NORM_CHUNK = 64


def _norm_kernel(x_ref, g_ref, sc_ref, sh_ref, o_ref):
    gain = g_ref[0] * (1.0 + sc_ref[0])
    shift = sh_ref[0]

    def chunk(c, carry):
        rows = pl.ds(pl.multiple_of(c * NORM_CHUNK, NORM_CHUNK), NORM_CHUNK)
        x = x_ref[rows, :]
        r = lax.rsqrt(jnp.mean(x * x, axis=-1, keepdims=True) + EPS)
        o_ref[rows, :] = ((x * r) * gain + shift).astype(o_ref.dtype)
        return carry

    lax.fori_loop(0, x_ref.shape[0] // NORM_CHUNK, chunk, 0)


def _mod_norm(x, gain, layer, scale, shift):
    tm = 1024
    per_batch = SEQ // tm
    row = lambda i: (i // per_batch, 0, 0)
    return pl.pallas_call(
        _norm_kernel,
        grid=(ROWS // tm,),
        in_specs=[
            pl.BlockSpec((tm, D_MODEL), lambda i: (i, 0)),
            pl.BlockSpec((1, 1, D_MODEL), lambda i: (layer, 0, 0)),
            pl.BlockSpec((1, 1, D_MODEL), row),
            pl.BlockSpec((1, 1, D_MODEL), row),
        ],
        out_specs=pl.BlockSpec((tm, D_MODEL), lambda i: (i, 0)),
        out_shape=jax.ShapeDtypeStruct((ROWS, D_MODEL), BF16),
        compiler_params=_params("arbitrary"),
        name="mod_norm",
    )(x, gain.reshape(DEPTH, 1, D_MODEL), scale, shift)


def _final_norm_kernel(x_ref, g_ref, o_ref):
    gain = g_ref[...]

    def chunk(c, carry):
        rows = pl.ds(pl.multiple_of(c * NORM_CHUNK, NORM_CHUNK), NORM_CHUNK)
        x = x_ref[rows, :]
        r = lax.rsqrt(jnp.mean(x * x, axis=-1, keepdims=True) + EPS)
        o_ref[rows, :] = x * r * gain
        return carry

    lax.fori_loop(0, x_ref.shape[0] // NORM_CHUNK, chunk, 0)


def _final_norm(x, gain):
    tm = 512
    return pl.pallas_call(
        _final_norm_kernel,
        grid=(ROWS // tm,),
        in_specs=[
            pl.BlockSpec((tm, D_MODEL), lambda i: (i, 0)),
            pl.BlockSpec((1, D_MODEL), lambda i: (0, 0)),
        ],
        out_specs=pl.BlockSpec((tm, D_MODEL), lambda i: (i, 0)),
        out_shape=jax.ShapeDtypeStruct((ROWS, D_MODEL), F32),
        compiler_params=_params("arbitrary"),
        name="final_norm",
    )(x, gain.reshape(1, D_MODEL))


def _cast_chunk(w_ref, scr, slot, step, col0=0):
    rows, cols = w_ref.shape[1], w_ref.shape[2]
    r0 = pl.multiple_of(step * rows, rows)
    scr[slot, pl.ds(r0, rows), col0:col0 + cols] = w_ref[0].astype(BF16)


def _ws_ids():
    j = pl.program_id(0)
    return j, pl.program_id(1), j % 2


def _w_spec(layer, k, ni, cols, n_tiles, col_tile0=0):
    return pl.BlockSpec((1, k // ni, cols),
                        lambda j, i: (layer, i, col_tile0 + jnp.minimum(j, n_tiles - 1)))


def _a_spec(tm, k, col_block=0):
    return pl.BlockSpec((tm, k), lambda j, i: (jnp.where(j > 0, i, 0), col_block))


def _o_spec(tm, tn):
    return pl.BlockSpec((tm, tn), lambda j, i: (jnp.where(j > 0, i, 0), jnp.maximum(j - 1, 0)))


PROJ_TN = 512
PROJ_U_TILES = POOL_WIDTH // PROJ_TN
PROJ_Q_TILES = Q_WIDTH // PROJ_TN
PROJ_KV_TILES = 2 * KV_WIDTH // PROJ_TN


def _proj_kernel(a_ref, w_ref, u_ref, q_ref, kv_ref, scr):
    j, i, slot = _ws_ids()
    q_start = 1 + PROJ_U_TILES
    kv_start = q_start + PROJ_Q_TILES

    @pl.when(j == 0)
    def _():
        _cast_chunk(w_ref, scr, slot, i)

    @pl.when((j >= 1) & (j < q_start))
    def _():
        u_ref[...] = _dot(a_ref[...], scr[1 - slot])
        _cast_chunk(w_ref, scr, slot, i)

    @pl.when((j >= q_start) & (j < kv_start))
    def _():
        q_ref[...] = (_dot(a_ref[...], scr[1 - slot]) * HEAD_DIM ** -0.5).astype(q_ref.dtype)
        _cast_chunk(w_ref, scr, slot, i)

    @pl.when(j >= kv_start)
    def _():
        kv_ref[...] = _dot(a_ref[...], scr[1 - slot]).astype(kv_ref.dtype)
        _cast_chunk(w_ref, scr, slot, i)


def _proj_out_spec(tm, ni, first, n_tiles):
    def index(j, i):
        row = jnp.where(j < first, 0, jnp.where(j < first + n_tiles, i, ni - 1))
        return row, jnp.clip(j - first, 0, n_tiles - 1)
    return pl.BlockSpec((tm, PROJ_TN), index)


def _proj(a, w, layer, tm=1024):
    m, k = a.shape
    ni, nj = m // tm, IN_WIDTH // PROJ_TN
    q_start = 1 + PROJ_U_TILES
    return pl.pallas_call(
        _proj_kernel,
        grid=(nj + 1, ni),
        in_specs=[_a_spec(tm, k), _w_spec(layer, k, ni, PROJ_TN, nj)],
        out_specs=[_proj_out_spec(tm, ni, 1, PROJ_U_TILES),
                   _proj_out_spec(tm, ni, q_start, PROJ_Q_TILES),
                   _proj_out_spec(tm, ni, q_start + PROJ_Q_TILES, PROJ_KV_TILES)],
        out_shape=[jax.ShapeDtypeStruct((m, POOL_WIDTH), F32),
                   jax.ShapeDtypeStruct((m, Q_WIDTH), BF16),
                   jax.ShapeDtypeStruct((m, 2 * KV_WIDTH), BF16)],
        scratch_shapes=[pltpu.VMEM((2, k, PROJ_TN), BF16)],
        compiler_params=_params("arbitrary", "arbitrary"),
        name="proj",
    )(a, w)


POOL_PAD = 16


def _pool_kernel(u_ref, w_ref, s_ref, *rest, with_mod):
    g = pl.program_id(1)
    if with_mod:
        c_ref, wada_ref, partial_ref, o_ref, mod_ref, a_ref, b_ref = rest

        @pl.when((pl.program_id(0) == 0) & (g == 0))
        def _():
            mod_ref[...] = partial_ref[...]

        mod_ref[...] += _dot(_silu_bf16(c_ref[0]), wada_ref[0].astype(BF16))
    else:
        o_ref, a_ref, b_ref = rest
    zeros = jnp.zeros((POOL_PAD, POOL_GROUP), F32)
    a_ref[0:POOL_PAD, :] = zeros
    b_ref[0:POOL_PAD, :] = zeros
    body = pl.ds(POOL_PAD, SEQ)
    a_ref[body, :] = u_ref[...]
    row = lax.broadcasted_iota(jnp.int32, (SEQ, 1), 0)

    def lagged(ref, lag):
        return ref[pl.ds(POOL_PAD - lag, SEQ), :]

    for group in range(N_GROUPS):
        @pl.when(g == group)
        def _():
            src, dst = a_ref, b_ref
            for level in range(group):
                dst[body, :] = src[body, :] + lagged(src, 1 << level)
                src, dst = dst, src
            total = src[body, :] + lagged(src, 1 << group)
            inv_count = 1.0 / jnp.minimum(row + 1, POOL_WINDOWS[group]).astype(F32)
            pooled = total * inv_count - u_ref[...]
            y = _dot(pooled.astype(BF16), w_ref[0, 0].astype(BF16))
            o_ref[...] = (y * s_ref[0]).astype(o_ref.dtype)


def _pool_branch(u, w_mix, scale, layer, cond_slabs=None, w_ada=None, mod_partial=None):
    with_mod = cond_slabs is not None
    in_specs = [
        pl.BlockSpec((SEQ, POOL_GROUP), lambda b, g: (b, g)),
        pl.BlockSpec((1, 1, POOL_GROUP, POOL_GROUP), lambda b, g: (layer, g, 0, 0)),
        pl.BlockSpec((1, 1, POOL_GROUP), lambda b, g: (layer, 0, g)),
    ]
    out_specs = [pl.BlockSpec((SEQ, POOL_GROUP), lambda b, g: (b, g))]
    out_shape = [jax.ShapeDtypeStruct((ROWS, POOL_WIDTH), BF16)]
    args = [u, w_mix, scale.reshape(DEPTH, 1, POOL_WIDTH)]
    if with_mod:
        first_slab = ADA_ATTN_ROWS // ADA_POOL_ROWS
        in_specs += [
            pl.BlockSpec((1, COND_ROWS, ADA_POOL_ROWS), lambda b, g: (b * N_GROUPS + g, 0, 0)),
            pl.BlockSpec((1, ADA_POOL_ROWS, N_ADA), lambda b, g: (layer + 1, first_slab + b * N_GROUPS + g, 0)),
            pl.BlockSpec((COND_ROWS, N_ADA), lambda b, g: (0, 0)),
        ]
        out_specs.append(pl.BlockSpec((COND_ROWS, N_ADA), lambda b, g: (0, 0)))
        out_shape.append(jax.ShapeDtypeStruct((COND_ROWS, N_ADA), F32))
        args += [cond_slabs, w_ada, mod_partial]
    return pl.pallas_call(
        functools.partial(_pool_kernel, with_mod=with_mod),
        grid=(BATCH, N_GROUPS),
        in_specs=in_specs,
        out_specs=out_specs,
        out_shape=out_shape,
        scratch_shapes=[pltpu.VMEM((POOL_PAD + SEQ, POOL_GROUP), F32),
                        pltpu.VMEM((POOL_PAD + SEQ, POOL_GROUP), F32)],
        compiler_params=_params("arbitrary", "arbitrary"),
        name="pool_branch",
    )(*args)


BAND = 2 * BLOCK
PAIRS = Q_PER_KV // 2


def _bias_kernel(rel_ref, bucket_ref, o_ref):
    h = pl.program_id(1)
    bucket = bucket_ref[0]
    for p in range(PAIRS):
        for e in range(2):
            head = h * Q_PER_KV + 2 * p + e
            acc = jnp.full(bucket.shape, NEG, F32)
            for b in range(N_BUCKETS):
                acc = jnp.where(bucket == b, rel_ref[b, head], acc)
            r0 = ((p // 2) * 2 + e) * BAND
            c0 = (p % 2) * BLOCK
            o_ref[0, 0, r0:r0 + BAND, c0:c0 + BLOCK] = acc


def _band_bias(rel_bias, bucket_pair):
    return pl.pallas_call(
        _bias_kernel,
        grid=(2, N_KV_HEADS),
        in_specs=[
            pl.BlockSpec(memory_space=pltpu.SMEM),
            pl.BlockSpec((1, BAND, BLOCK), lambda f, h: (f, 0, 0)),
        ],
        out_specs=pl.BlockSpec((1, 1, PAIRS * BAND, 2 * BLOCK), lambda f, h: (f, h, 0, 0)),
        out_shape=jax.ShapeDtypeStruct((2, N_KV_HEADS, PAIRS * BAND, 2 * BLOCK), F32),
        compiler_params=_params("arbitrary", "arbitrary"),
        name="band_bias",
    )(rel_bias, bucket_pair)


def _bucket_tables():
    i = jnp.arange(BLOCK)[None, :]
    j = jnp.arange(BAND)[:, None]
    dist = i + BLOCK - j
    max_exact = N_BUCKETS // 2
    d = jnp.maximum(dist, 0)
    log_ratio = jnp.log(jnp.maximum(d, 1).astype(F32) / max_exact) / math.log(MAX_DISTANCE / max_exact)
    large = jnp.minimum(max_exact + (log_ratio * (N_BUCKETS - max_exact)).astype(jnp.int32), N_BUCKETS - 1)
    bucket = jnp.where(d < max_exact, d, large)
    in_window = (dist >= 0) & (dist < WINDOW)
    later = jnp.where(in_window, bucket, -1)
    first = jnp.where(j >= BLOCK, later, -1)
    return jnp.stack([first, later]).astype(jnp.int32)


N_ATTN_STEPS = ROWS // BLOCK
ADA_SIDE_ROWS = 48
ADA_ATTN_ROWS = N_ATTN_STEPS * ADA_SIDE_ROWS
N_POOL_STEPS = BATCH * N_GROUPS
ADA_POOL_ROWS = (D_MODEL - ADA_ATTN_ROWS) // N_POOL_STEPS
assert ADA_ATTN_ROWS + N_POOL_STEPS * ADA_POOL_ROWS == D_MODEL and ADA_ATTN_ROWS % ADA_POOL_ROWS == 0
N_ATTN_GROUPS = N_KV_HEADS * (PAIRS // 2)
ADA_SIDE_COLS = N_ADA // N_ATTN_GROUPS


def _attn_kernel(sink_ref, q_ref, kvp_ref, kvc_ref, bias_ref, *rest, layer, with_mod):
    if with_mod:
        c_ref, wada_ref, bada_ref, o_ref, mod_ref = rest

        @pl.when(pl.program_id(0) == 0)
        def _():
            mod_ref[...] = jnp.broadcast_to(bada_ref[0], mod_ref.shape)

        cond = _silu_bf16(c_ref[0])

        def next_mod(n):
            cols = slice(n * ADA_SIDE_COLS, (n + 1) * ADA_SIDE_COLS)
            mod_ref[:, cols] += _dot(cond, wada_ref[0, :, cols].astype(BF16))
    else:
        o_ref, = rest

        def next_mod(n):
            pass

    lane = lax.broadcasted_iota(jnp.int32, (1, 2 * HEAD_DIM), 1)
    low = lane < HEAD_DIM
    left = lax.broadcasted_iota(jnp.int32, (1, 2 * BLOCK), 1) < BLOCK
    ones = jnp.ones((V7X_BF16_SUBLANES, BAND), BF16)
    nt = (((1,), (1,)), ((), ()))
    keys, values = [], []
    for slab in range(N_KV_HEADS // 2):
        c = slab * 2 * HEAD_DIM
        k_slab = jnp.concatenate([kvp_ref[:, c:c + 2 * HEAD_DIM], kvc_ref[:, c:c + 2 * HEAD_DIM]], axis=0)
        v_slab = jnp.concatenate([kvp_ref[:, KV_WIDTH + c:KV_WIDTH + c + 2 * HEAD_DIM],
                                  kvc_ref[:, KV_WIDTH + c:KV_WIDTH + c + 2 * HEAD_DIM]], axis=0)
        k_swap = pltpu.roll(k_slab, HEAD_DIM, axis=1)
        v_t = v_slab.astype(F32).T.astype(BF16)
        zero = jnp.zeros_like(k_slab)
        for hh in range(2):
            k_on_low, k_on_high = (k_slab, k_swap) if hh == 0 else (k_swap, k_slab)
            keys.append((jnp.where(low, k_on_low, zero), jnp.where(low, zero, k_on_high)))
            values.append(jnp.concatenate([v_t[hh * HEAD_DIM:(hh + 1) * HEAD_DIM], ones], axis=0))

    def scores(h, t):
        c0 = h * Q_PER_KV * HEAD_DIM + t * 2 * BLOCK
        q2 = jnp.concatenate([q_ref[:, c0:c0 + BLOCK], q_ref[:, c0 + BLOCK:c0 + 2 * BLOCK]], axis=0)
        return [lax.dot_general(keys[h][e], q2, nt, preferred_element_type=F32)
                + bias_ref[0, h, (t * 2 + e) * BAND:(t * 2 + e + 1) * BAND, :] for e in range(2)]

    def softmax(h, t, s_pair):
        probs, tails = [], []
        for e in range(2):
            head0 = h * Q_PER_KV + 4 * t + e
            sink = jnp.where(left, sink_ref[layer, head0], sink_ref[layer, head0 + 2])
            m = jnp.maximum(jnp.max(s_pair[e], axis=0, keepdims=True), sink)
            probs.append(jnp.exp(s_pair[e] - m).astype(BF16))
            tails.append(jnp.exp(sink - m))
        return probs, tails

    def finish(h, t, probs, tails):
        for pp in range(2):
            cols = slice(pp * BLOCK, (pp + 1) * BLOCK)
            p_t = jnp.concatenate([probs[0][:, cols], probs[1][:, cols]], axis=1)
            tail = jnp.concatenate([tails[0][:, cols], tails[1][:, cols]], axis=1)
            res = _dot(values[h], p_t)
            out_t = res[:HEAD_DIM] * (1.0 / (res[HEAD_DIM:HEAD_DIM + 1] + tail))
            pair_t = jnp.concatenate([out_t[:, :BLOCK], out_t[:, BLOCK:]], axis=0)
            o0 = h * Q_PER_KV * HEAD_DIM + (2 * t + pp) * BLOCK
            o_ref[:, o0:o0 + BLOCK] = pair_t.T.astype(o_ref.dtype)

    groups = [(h, t) for h in range(N_KV_HEADS) for t in range(PAIRS // 2)]
    pending = scores(*groups[0])
    ready = None
    for n, group in enumerate(groups):
        current = pending
        if n + 1 < len(groups):
            pending = scores(*groups[n + 1])
        weights = softmax(*group, current)
        next_mod(n)
        if ready is not None:
            finish(*groups[n - 1], *ready)
        ready = weights
    finish(*groups[-1], *ready)


def _attention(q, kv, sinks, bias, layer, cond_slabs=None, w_ada=None, b_ada=None):
    with_mod = cond_slabs is not None
    in_specs = [
        pl.BlockSpec(memory_space=pltpu.SMEM),
        pl.BlockSpec((BLOCK, Q_WIDTH), lambda r: (r, 0)),
        pl.BlockSpec((BLOCK, 2 * KV_WIDTH), lambda r: (jnp.maximum(r - 1, 0), 0)),
        pl.BlockSpec((BLOCK, 2 * KV_WIDTH), lambda r: (r, 0)),
        pl.BlockSpec((1, N_KV_HEADS, PAIRS * BAND, 2 * BLOCK),
                     lambda r: (jnp.minimum(r % N_BLOCKS, 1), 0, 0, 0)),
    ]
    out_specs = [pl.BlockSpec((BLOCK, Q_WIDTH), lambda r: (r, 0))]
    out_shape = [jax.ShapeDtypeStruct((ROWS, Q_WIDTH), BF16)]
    args = [sinks, q, kv, kv, bias]
    if with_mod:
        in_specs += [
            pl.BlockSpec((1, COND_ROWS, ADA_SIDE_ROWS), lambda r: (r, 0, 0)),
            pl.BlockSpec((1, ADA_SIDE_ROWS, N_ADA), lambda r: (layer + 1, r, 0)),
            pl.BlockSpec((1, 1, N_ADA), lambda r: (layer + 1, 0, 0)),
        ]
        out_specs.append(pl.BlockSpec((COND_ROWS, N_ADA), lambda r: (0, 0)))
        out_shape.append(jax.ShapeDtypeStruct((COND_ROWS, N_ADA), F32))
        args += [cond_slabs, w_ada, b_ada.reshape(DEPTH, 1, N_ADA)]
    return pl.pallas_call(
        functools.partial(_attn_kernel, layer=layer, with_mod=with_mod),
        grid=(N_ATTN_STEPS,),
        in_specs=in_specs,
        out_specs=out_specs,
        out_shape=out_shape,
        compiler_params=_params("arbitrary"),
        name="swa_attention",
    )(*args)


def _merge_kernel(h_ref, yp_ref, ya_ref, wgp_ref, wga_ref, wbp_ref, wba_ref, o_ref,
                  sgp, sga, sbp, sba):
    j, i, slot = _ws_ids()

    def cast():
        _cast_chunk(wgp_ref, sgp, slot, i)
        _cast_chunk(wga_ref, sga, slot, i)
        _cast_chunk(wbp_ref, sbp, slot, i)
        _cast_chunk(wba_ref, sba, slot, i)

    @pl.when(j == 0)
    def _():
        cast()

    @pl.when(j > 0)
    def _():
        h = h_ref[...]
        gate_pool = jax.nn.sigmoid(_dot(h, sgp[1 - slot]))
        gate_attn = jax.nn.sigmoid(_dot(h, sga[1 - slot]))
        y_pool = _dot(yp_ref[...], sbp[1 - slot])
        y_attn = _dot(ya_ref[...], sba[1 - slot])
        o_ref[...] = (gate_pool * y_pool + gate_attn * y_attn).astype(o_ref.dtype)
        cast()


def _merge(h, yp, ya, w_gate, w_bp, w_ba, layer, tm=512, tn=512):
    ni, nj = ROWS // tm, D_MODEL // tn
    return pl.pallas_call(
        _merge_kernel,
        grid=(nj + 1, ni),
        in_specs=[
            _a_spec(tm, D_MODEL), _a_spec(tm, POOL_WIDTH), _a_spec(tm, Q_WIDTH),
            _w_spec(layer, D_MODEL, ni, tn, nj),
            _w_spec(layer, D_MODEL, ni, tn, nj, col_tile0=nj),
            _w_spec(layer, POOL_WIDTH, ni, tn, nj),
            _w_spec(layer, Q_WIDTH, ni, tn, nj),
        ],
        out_specs=_o_spec(tm, tn),
        out_shape=jax.ShapeDtypeStruct((ROWS, D_MODEL), BF16),
        scratch_shapes=[pltpu.VMEM((2, D_MODEL, tn), BF16), pltpu.VMEM((2, D_MODEL, tn), BF16),
                        pltpu.VMEM((2, POOL_WIDTH, tn), BF16), pltpu.VMEM((2, Q_WIDTH, tn), BF16)],
        compiler_params=_params("arbitrary", "arbitrary"),
        name="gated_merge",
    )(h, yp, ya, w_gate, w_gate, w_bp, w_ba)


def _resid_kernel(a_ref, w_ref, x_ref, g_ref, o_ref, scr):
    j, i, slot = _ws_ids()

    @pl.when(j == 0)
    def _():
        _cast_chunk(w_ref, scr, slot, i)

    @pl.when(j > 0)
    def _():
        o_ref[...] = x_ref[...] + g_ref[0] * _dot(a_ref[...], scr[1 - slot])
        _cast_chunk(w_ref, scr, slot, i)


def _resid_proj(a, w, x, gate, layer, tm, tn):
    m, k = a.shape
    n = x.shape[1]
    ni, nj = m // tm, n // tn
    per_batch = SEQ // tm
    return pl.pallas_call(
        _resid_kernel,
        grid=(nj + 1, ni),
        in_specs=[
            _a_spec(tm, k),
            _w_spec(layer, k, ni, tn, nj),
            _o_spec(tm, tn),
            pl.BlockSpec((1, 1, tn), lambda j, i: (i // per_batch, 0, jnp.maximum(j - 1, 0))),
        ],
        out_specs=_o_spec(tm, tn),
        out_shape=jax.ShapeDtypeStruct((m, n), F32),
        scratch_shapes=[pltpu.VMEM((2, k, tn), BF16)],
        compiler_params=_params("arbitrary", "arbitrary"),
        name="resid_proj",
    )(a, w, x, gate)


FF_TILE = V7X_MXU_DIM
SWIGLU_ROWS = 1024


def _swiglu_kernel(a_ref, wa_ref, wb_ref, o_ref, scr):
    j, i, slot = _ws_ids()

    def cast():
        _cast_chunk(wa_ref, scr, slot, i)
        _cast_chunk(wb_ref, scr, slot, i, col0=FF_TILE)

    @pl.when(j == 0)
    def _():
        cast()

    @pl.when(j > 0)
    def _():
        w = scr[1 - slot]
        for r0 in range(0, a_ref.shape[0], SWIGLU_ROWS):
            ab = _dot(a_ref[r0:r0 + SWIGLU_ROWS, :], w)
            a = ab[:, :FF_TILE]
            b = ab[:, FF_TILE:]
            o_ref[r0:r0 + SWIGLU_ROWS, :] = ((a * jax.nn.sigmoid(a)) * b).astype(o_ref.dtype)
        cast()


def _swiglu(h2, w_ffn_in, layer, tm=2048):
    ni, nj = ROWS // tm, D_FF // FF_TILE
    return pl.pallas_call(
        _swiglu_kernel,
        grid=(nj + 1, ni),
        in_specs=[
            _a_spec(tm, D_MODEL),
            _w_spec(layer, D_MODEL, ni, FF_TILE, nj),
            _w_spec(layer, D_MODEL, ni, FF_TILE, nj, col_tile0=nj),
        ],
        out_specs=_o_spec(tm, FF_TILE),
        out_shape=jax.ShapeDtypeStruct((ROWS, D_FF), BF16),
        scratch_shapes=[pltpu.VMEM((2, D_MODEL, 2 * FF_TILE), BF16)],
        compiler_params=_params("arbitrary", "arbitrary"),
        name="swiglu_up",
    )(h2, w_ffn_in, w_ffn_in)


def kernel(x, c, w_ada, b_ada, norm1, w_in, w_pool_mix, pool_scale, sinks, rel_bias,
           w_branch_pool, w_branch_attn, w_gate, w_out, norm2, w_ffn_in, w_ffn_out,
           final_norm):
    c_pad = jnp.pad(c, ((0, COND_ROWS - BATCH), (0, 0)))
    cond_attn = c_pad[:, :ADA_ATTN_ROWS].reshape(COND_ROWS, N_ATTN_STEPS, ADA_SIDE_ROWS).transpose(1, 0, 2)
    cond_pool = c_pad[:, ADA_ATTN_ROWS:].reshape(COND_ROWS, N_POOL_STEPS, ADA_POOL_ROWS).transpose(1, 0, 2)
    mod = _ada_first(c_pad, w_ada, b_ada)
    bias = _band_bias(rel_bias, _bucket_tables())
    xr = x.reshape(ROWS, D_MODEL)

    for l in range(DEPTH):
        sh1, sc1, g1, sh2, sc2, g2 = [m.reshape(BATCH, 1, D_MODEL)
                                      for m in jnp.split(mod[:BATCH], N_MOD, axis=-1)]
        h = _mod_norm(xr, norm1, l, sc1, sh1)
        u, q, kv = _proj(h, w_in, l)
        if l + 1 < DEPTH:
            ya, mod_partial = _attention(q, kv, sinks, bias, l, cond_attn, w_ada, b_ada)
            yp, mod = _pool_branch(u, w_pool_mix, pool_scale, l, cond_pool, w_ada, mod_partial)
        else:
            ya, = _attention(q, kv, sinks, bias, l)
            yp, = _pool_branch(u, w_pool_mix, pool_scale, l)
        merged = _merge(h, yp, ya, w_gate, w_branch_pool, w_branch_attn, l)
        xr = _resid_proj(merged, w_out, xr, g1, l, tm=512, tn=1024)

        h2 = _mod_norm(xr, norm2, l, sc2, sh2)
        act = _swiglu(h2, w_ffn_in, l)
        xr = _resid_proj(act, w_ffn_out, xr, g2, l, tm=512, tn=512)

    return _final_norm(xr, final_norm).reshape(BATCH, SEQ, D_MODEL)
```

```python
import functools
import math

import jax
import jax.numpy as jnp
from jax import lax
from jax.experimental import pallas as pl
from jax.experimental.pallas import tpu as pltpu

D_MODEL = 4096
BATCH = 4
SEQ = 2048
ROWS = BATCH * SEQ
DEPTH = 2
POOL_WINDOWS = (2, 4, 8, 16)
N_GROUPS = len(POOL_WINDOWS)
POOL_WIDTH = D_MODEL // 2
POOL_GROUP = POOL_WIDTH // N_GROUPS
HEAD_DIM = 64
N_Q_HEADS = (D_MODEL // 2) // HEAD_DIM
N_KV_HEADS = N_Q_HEADS // 8
Q_PER_KV = N_Q_HEADS // N_KV_HEADS
Q_WIDTH = N_Q_HEADS * HEAD_DIM
KV_WIDTH = N_KV_HEADS * HEAD_DIM
WINDOW = 128
BLOCK = 128
N_BLOCKS = SEQ // BLOCK
N_BUCKETS = 32
MAX_DISTANCE = 128
IN_WIDTH = POOL_WIDTH + Q_WIDTH + 2 * KV_WIDTH
D_FF = -(-(8 * D_MODEL) // (3 * 256)) * 256
N_MOD = 6
EPS = 1e-6

V7X_VMEM_BYTES = 64 * 1024 * 1024
VMEM_LIMIT = V7X_VMEM_BYTES - 8 * 1024 * 1024
V7X_MXU_DIM = 256
V7X_SUBLANES = 8
V7X_BF16_SUBLANES = 16
COND_ROWS = V7X_SUBLANES

NEG = -1e30

BF16 = jnp.bfloat16
F32 = jnp.float32


def _params(*sem):
    return pltpu.CompilerParams(dimension_semantics=sem, vmem_limit_bytes=VMEM_LIMIT)


def _dot(a, b):
    return jnp.dot(a, b, preferred_element_type=F32)


ADA_ROWS = 128
N_ADA = N_MOD * D_MODEL


def _silu_bf16(c):
    return (c * jax.nn.sigmoid(c)).astype(BF16)


def _ada_kernel(c_ref, w_ref, b_ref, o_ref):
    k = pl.program_id(0)

    @pl.when(k == 0)
    def _():
        o_ref[...] = jnp.broadcast_to(b_ref[0], o_ref.shape)

    o_ref[...] += _dot(_silu_bf16(c_ref[...]), w_ref[0].astype(BF16))


def _ada_first(c_pad, w_ada, b_ada):
    return pl.pallas_call(
        _ada_kernel,
        grid=(D_MODEL // ADA_ROWS,),
        in_specs=[
            pl.BlockSpec((COND_ROWS, ADA_ROWS), lambda k: (0, k)),
            pl.BlockSpec((1, ADA_ROWS, N_ADA), lambda k: (0, k, 0)),
            pl.BlockSpec((1, 1, N_ADA), lambda k: (0, 0, 0)),
        ],
        out_specs=pl.BlockSpec((COND_ROWS, N_ADA), lambda k: (0, 0)),
        out_shape=jax.ShapeDtypeStruct((COND_ROWS, N_ADA), F32),
        compiler_params=_params("arbitrary"),
        name="ada_mod",
    )(c_pad, w_ada, b_ada.reshape(DEPTH, 1, N_ADA))


NORM_CHUNK = 64


def _norm_kernel(x_ref, g_ref, sc_ref, sh_ref, o_ref):
    gain = g_ref[0] * (1.0 + sc_ref[0])
    shift = sh_ref[0]

    def chunk(c, carry):
        rows = pl.ds(pl.multiple_of(c * NORM_CHUNK, NORM_CHUNK), NORM_CHUNK)
        x = x_ref[rows, :]
        r = lax.rsqrt(jnp.mean(x * x, axis=-1, keepdims=True) + EPS)
        o_ref[rows, :] = ((x * r) * gain + shift).astype(o_ref.dtype)
        return carry

    lax.fori_loop(0, x_ref.shape[0] // NORM_CHUNK, chunk, 0)


def _mod_norm(x, gain, layer, scale, shift):
    tm = 1024
    per_batch = SEQ // tm
    row = lambda i: (i // per_batch, 0, 0)
    return pl.pallas_call(
        _norm_kernel,
        grid=(ROWS // tm,),
        in_specs=[
            pl.BlockSpec((tm, D_MODEL), lambda i: (i, 0)),
            pl.BlockSpec((1, 1, D_MODEL), lambda i: (layer, 0, 0)),
            pl.BlockSpec((1, 1, D_MODEL), row),
            pl.BlockSpec((1, 1, D_MODEL), row),
        ],
        out_specs=pl.BlockSpec((tm, D_MODEL), lambda i: (i, 0)),
        out_shape=jax.ShapeDtypeStruct((ROWS, D_MODEL), BF16),
        compiler_params=_params("arbitrary"),
        name="mod_norm",
    )(x, gain.reshape(DEPTH, 1, D_MODEL), scale, shift)


def _final_norm_kernel(x_ref, g_ref, o_ref):
    gain = g_ref[...]

    def chunk(c, carry):
        rows = pl.ds(pl.multiple_of(c * NORM_CHUNK, NORM_CHUNK), NORM_CHUNK)
        x = x_ref[rows, :]
        r = lax.rsqrt(jnp.mean(x * x, axis=-1, keepdims=True) + EPS)
        o_ref[rows, :] = x * r * gain
        return carry

    lax.fori_loop(0, x_ref.shape[0] // NORM_CHUNK, chunk, 0)


def _final_norm(x, gain):
    tm = 512
    return pl.pallas_call(
        _final_norm_kernel,
        grid=(ROWS // tm,),
        in_specs=[
            pl.BlockSpec((tm, D_MODEL), lambda i: (i, 0)),
            pl.BlockSpec((1, D_MODEL), lambda i: (0, 0)),
        ],
        out_specs=pl.BlockSpec((tm, D_MODEL), lambda i: (i, 0)),
        out_shape=jax.ShapeDtypeStruct((ROWS, D_MODEL), F32),
        compiler_params=_params("arbitrary"),
        name="final_norm",
    )(x, gain.reshape(1, D_MODEL))


def _cast_chunk(w_ref, scr, slot, step, col0=0):
    rows, cols = w_ref.shape[1], w_ref.shape[2]
    r0 = pl.multiple_of(step * rows, rows)
    scr[slot, pl.ds(r0, rows), col0:col0 + cols] = w_ref[0].astype(BF16)


def _ws_ids():
    j = pl.program_id(0)
    return j, pl.program_id(1), j % 2


def _w_spec(layer, k, ni, cols, n_tiles, col_tile0=0):
    return pl.BlockSpec((1, k // ni, cols),
                        lambda j, i: (layer, i, col_tile0 + jnp.minimum(j, n_tiles - 1)))


def _a_spec(tm, k, col_block=0):
    return pl.BlockSpec((tm, k), lambda j, i: (jnp.where(j > 0, i, 0), col_block))


def _o_spec(tm, tn):
    return pl.BlockSpec((tm, tn), lambda j, i: (jnp.where(j > 0, i, 0), jnp.maximum(j - 1, 0)))


PROJ_TN = 512
PROJ_U_TILES = POOL_WIDTH // PROJ_TN
PROJ_Q_TILES = Q_WIDTH // PROJ_TN
PROJ_KV_TILES = 2 * KV_WIDTH // PROJ_TN


def _proj_kernel(a_ref, w_ref, u_ref, q_ref, kv_ref, scr):
    j, i, slot = _ws_ids()
    q_start = 1 + PROJ_U_TILES
    kv_start = q_start + PROJ_Q_TILES

    @pl.when(j == 0)
    def _():
        _cast_chunk(w_ref, scr, slot, i)

    @pl.when((j >= 1) & (j < q_start))
    def _():
        u_ref[...] = _dot(a_ref[...], scr[1 - slot])
        _cast_chunk(w_ref, scr, slot, i)

    @pl.when((j >= q_start) & (j < kv_start))
    def _():
        q_ref[...] = (_dot(a_ref[...], scr[1 - slot]) * HEAD_DIM ** -0.5).astype(q_ref.dtype)
        _cast_chunk(w_ref, scr, slot, i)

    @pl.when(j >= kv_start)
    def _():
        kv_ref[...] = _dot(a_ref[...], scr[1 - slot]).astype(kv_ref.dtype)
        _cast_chunk(w_ref, scr, slot, i)


def _proj_out_spec(tm, ni, first, n_tiles):
    def index(j, i):
        row = jnp.where(j < first, 0, jnp.where(j < first + n_tiles, i, ni - 1))
        return row, jnp.clip(j - first, 0, n_tiles - 1)
    return pl.BlockSpec((tm, PROJ_TN), index)


def _proj(a, w, layer, tm=1024):
    m, k = a.shape
    ni, nj = m // tm, IN_WIDTH // PROJ_TN
    q_start = 1 + PROJ_U_TILES
    return pl.pallas_call(
        _proj_kernel,
        grid=(nj + 1, ni),
        in_specs=[_a_spec(tm, k), _w_spec(layer, k, ni, PROJ_TN, nj)],
        out_specs=[_proj_out_spec(tm, ni, 1, PROJ_U_TILES),
                   _proj_out_spec(tm, ni, q_start, PROJ_Q_TILES),
                   _proj_out_spec(tm, ni, q_start + PROJ_Q_TILES, PROJ_KV_TILES)],
        out_shape=[jax.ShapeDtypeStruct((m, POOL_WIDTH), F32),
                   jax.ShapeDtypeStruct((m, Q_WIDTH), BF16),
                   jax.ShapeDtypeStruct((m, 2 * KV_WIDTH), BF16)],
        scratch_shapes=[pltpu.VMEM((2, k, PROJ_TN), BF16)],
        compiler_params=_params("arbitrary", "arbitrary"),
        name="proj",
    )(a, w)


POOL_PAD = 16


def _pool_kernel(u_ref, w_ref, s_ref, o_ref, a_ref, b_ref):
    g = pl.program_id(1)
    zeros = jnp.zeros((POOL_PAD, POOL_GROUP), F32)
    a_ref[0:POOL_PAD, :] = zeros
    b_ref[0:POOL_PAD, :] = zeros
    body = pl.ds(POOL_PAD, SEQ)
    a_ref[body, :] = u_ref[...]
    row = lax.broadcasted_iota(jnp.int32, (SEQ, 1), 0)

    def lagged(ref, lag):
        return ref[pl.ds(POOL_PAD - lag, SEQ), :]

    for group in range(N_GROUPS):
        @pl.when(g == group)
        def _():
            src, dst = a_ref, b_ref
            for level in range(group):
                dst[body, :] = src[body, :] + lagged(src, 1 << level)
                src, dst = dst, src
            total = src[body, :] + lagged(src, 1 << group)
            inv_count = 1.0 / jnp.minimum(row + 1, POOL_WINDOWS[group]).astype(F32)
            pooled = total * inv_count - u_ref[...]
            y = _dot(pooled.astype(BF16), w_ref[0, 0].astype(BF16))
            o_ref[...] = (y * s_ref[0]).astype(o_ref.dtype)


def _pool_branch(u, w_mix, scale, layer):
    return pl.pallas_call(
        _pool_kernel,
        grid=(BATCH, N_GROUPS),
        in_specs=[
            pl.BlockSpec((SEQ, POOL_GROUP), lambda b, g: (b, g)),
            pl.BlockSpec((1, 1, POOL_GROUP, POOL_GROUP), lambda b, g: (layer, g, 0, 0)),
            pl.BlockSpec((1, 1, POOL_GROUP), lambda b, g: (layer, 0, g)),
        ],
        out_specs=pl.BlockSpec((SEQ, POOL_GROUP), lambda b, g: (b, g)),
        out_shape=jax.ShapeDtypeStruct((ROWS, POOL_WIDTH), BF16),
        scratch_shapes=[pltpu.VMEM((POOL_PAD + SEQ, POOL_GROUP), F32),
                        pltpu.VMEM((POOL_PAD + SEQ, POOL_GROUP), F32)],
        compiler_params=_params("arbitrary", "arbitrary"),
        name="pool_branch",
    )(u, w_mix, scale.reshape(DEPTH, 1, POOL_WIDTH))


BAND = 2 * BLOCK
PAIRS = Q_PER_KV // 2


def _bias_kernel(rel_ref, bucket_ref, o_ref):
    h = pl.program_id(1)
    bucket = bucket_ref[0]
    for p in range(PAIRS):
        for e in range(2):
            head = h * Q_PER_KV + 2 * p + e
            acc = jnp.full(bucket.shape, NEG, F32)
            for b in range(N_BUCKETS):
                acc = jnp.where(bucket == b, rel_ref[b, head], acc)
            r0 = ((p // 2) * 2 + e) * BAND
            c0 = (p % 2) * BLOCK
            o_ref[0, 0, r0:r0 + BAND, c0:c0 + BLOCK] = acc


def _band_bias(rel_bias, bucket_pair):
    return pl.pallas_call(
        _bias_kernel,
        grid=(2, N_KV_HEADS),
        in_specs=[
            pl.BlockSpec(memory_space=pltpu.SMEM),
            pl.BlockSpec((1, BAND, BLOCK), lambda f, h: (f, 0, 0)),
        ],
        out_specs=pl.BlockSpec((1, 1, PAIRS * BAND, 2 * BLOCK), lambda f, h: (f, h, 0, 0)),
        out_shape=jax.ShapeDtypeStruct((2, N_KV_HEADS, PAIRS * BAND, 2 * BLOCK), F32),
        compiler_params=_params("arbitrary", "arbitrary"),
        name="band_bias",
    )(rel_bias, bucket_pair)


def _bucket_tables():
    i = jnp.arange(BLOCK)[None, :]
    j = jnp.arange(BAND)[:, None]
    dist = i + BLOCK - j
    max_exact = N_BUCKETS // 2
    d = jnp.maximum(dist, 0)
    log_ratio = jnp.log(jnp.maximum(d, 1).astype(F32) / max_exact) / math.log(MAX_DISTANCE / max_exact)
    large = jnp.minimum(max_exact + (log_ratio * (N_BUCKETS - max_exact)).astype(jnp.int32), N_BUCKETS - 1)
    bucket = jnp.where(d < max_exact, d, large)
    in_window = (dist >= 0) & (dist < WINDOW)
    later = jnp.where(in_window, bucket, -1)
    first = jnp.where(j >= BLOCK, later, -1)
    return jnp.stack([first, later]).astype(jnp.int32)


N_ATTN_STEPS = ROWS // BLOCK
ADA_SIDE_ROWS = D_MODEL // N_ATTN_STEPS
ADA_SIDE_PARTS = 2
N_ATTN_GROUPS = N_KV_HEADS * (PAIRS // 2)
ADA_SIDE_COLS = N_ADA // N_ATTN_GROUPS


def _attn_kernel(sink_ref, q_ref, kvp_ref, kvc_ref, bias_ref, *rest, layer, with_mod):
    if with_mod:
        c_ref, *wada_refs, bada_ref, o_ref, mod_ref = rest

        @pl.when(pl.program_id(0) == 0)
        def _():
            mod_ref[...] = jnp.broadcast_to(bada_ref[0], mod_ref.shape)

        cond = _silu_bf16(c_ref[0])

        def next_mod(n):
            cols = slice(n * ADA_SIDE_COLS, (n + 1) * ADA_SIDE_COLS)
            slab = jnp.concatenate([w[0, :, cols].astype(BF16) for w in wada_refs], axis=0)
            mod_ref[:, cols] += _dot(cond, slab)
    else:
        o_ref, = rest

        def next_mod(n):
            pass

    lane = lax.broadcasted_iota(jnp.int32, (1, 2 * HEAD_DIM), 1)
    low = lane < HEAD_DIM
    left = lax.broadcasted_iota(jnp.int32, (1, 2 * BLOCK), 1) < BLOCK
    ones = jnp.ones((V7X_BF16_SUBLANES, BAND), BF16)
    nt = (((1,), (1,)), ((), ()))
    keys, values = [], []
    for slab in range(N_KV_HEADS // 2):
        c = slab * 2 * HEAD_DIM
        k_slab = jnp.concatenate([kvp_ref[:, c:c + 2 * HEAD_DIM], kvc_ref[:, c:c + 2 * HEAD_DIM]], axis=0)
        v_slab = jnp.concatenate([kvp_ref[:, KV_WIDTH + c:KV_WIDTH + c + 2 * HEAD_DIM],
                                  kvc_ref[:, KV_WIDTH + c:KV_WIDTH + c + 2 * HEAD_DIM]], axis=0)
        k_swap = pltpu.roll(k_slab, HEAD_DIM, axis=1)
        v_t = v_slab.astype(F32).T.astype(BF16)
        zero = jnp.zeros_like(k_slab)
        for hh in range(2):
            k_on_low, k_on_high = (k_slab, k_swap) if hh == 0 else (k_swap, k_slab)
            keys.append((jnp.where(low, k_on_low, zero), jnp.where(low, zero, k_on_high)))
            values.append(jnp.concatenate([v_t[hh * HEAD_DIM:(hh + 1) * HEAD_DIM], ones], axis=0))

    def scores(h, t):
        c0 = h * Q_PER_KV * HEAD_DIM + t * 2 * BLOCK
        q2 = jnp.concatenate([q_ref[:, c0:c0 + BLOCK], q_ref[:, c0 + BLOCK:c0 + 2 * BLOCK]], axis=0)
        return [lax.dot_general(keys[h][e], q2, nt, preferred_element_type=F32)
                + bias_ref[0, h, (t * 2 + e) * BAND:(t * 2 + e + 1) * BAND, :] for e in range(2)]

    def softmax(h, t, s_pair):
        probs, tails = [], []
        for e in range(2):
            head0 = h * Q_PER_KV + 4 * t + e
            sink = jnp.where(left, sink_ref[layer, head0], sink_ref[layer, head0 + 2])
            m = jnp.maximum(jnp.max(s_pair[e], axis=0, keepdims=True), sink)
            probs.append(jnp.exp(s_pair[e] - m).astype(BF16))
            tails.append(jnp.exp(sink - m))
        return probs, tails

    def finish(h, t, probs, tails):
        for pp in range(2):
            cols = slice(pp * BLOCK, (pp + 1) * BLOCK)
            p_t = jnp.concatenate([probs[0][:, cols], probs[1][:, cols]], axis=1)
            tail = jnp.concatenate([tails[0][:, cols], tails[1][:, cols]], axis=1)
            res = _dot(values[h], p_t)
            out_t = res[:HEAD_DIM] * (1.0 / (res[HEAD_DIM:HEAD_DIM + 1] + tail))
            pair_t = jnp.concatenate([out_t[:, :BLOCK], out_t[:, BLOCK:]], axis=0)
            o0 = h * Q_PER_KV * HEAD_DIM + (2 * t + pp) * BLOCK
            o_ref[:, o0:o0 + BLOCK] = pair_t.T.astype(o_ref.dtype)

    groups = [(h, t) for h in range(N_KV_HEADS) for t in range(PAIRS // 2)]
    pending = scores(*groups[0])
    ready = None
    for n, group in enumerate(groups):
        current = pending
        if n + 1 < len(groups):
            pending = scores(*groups[n + 1])
        weights = softmax(*group, current)
        next_mod(n)
        if ready is not None:
            finish(*groups[n - 1], *ready)
        ready = weights
    finish(*groups[-1], *ready)


def _attention(q, kv, sinks, bias, layer, cond_slabs=None, w_ada=None, b_ada=None):
    with_mod = cond_slabs is not None
    in_specs = [
        pl.BlockSpec(memory_space=pltpu.SMEM),
        pl.BlockSpec((BLOCK, Q_WIDTH), lambda r: (r, 0)),
        pl.BlockSpec((BLOCK, 2 * KV_WIDTH), lambda r: (jnp.maximum(r - 1, 0), 0)),
        pl.BlockSpec((BLOCK, 2 * KV_WIDTH), lambda r: (r, 0)),
        pl.BlockSpec((1, N_KV_HEADS, PAIRS * BAND, 2 * BLOCK),
                     lambda r: (jnp.minimum(r % N_BLOCKS, 1), 0, 0, 0)),
    ]
    out_specs = [pl.BlockSpec((BLOCK, Q_WIDTH), lambda r: (r, 0))]
    out_shape = [jax.ShapeDtypeStruct((ROWS, Q_WIDTH), BF16)]
    args = [sinks, q, kv, kv, bias]
    if with_mod:
        part_rows = ADA_SIDE_ROWS // ADA_SIDE_PARTS
        part_spec = lambda p: pl.BlockSpec((1, part_rows, N_ADA),
                                           lambda r: (layer + 1, r * ADA_SIDE_PARTS + p, 0))
        in_specs += [pl.BlockSpec((1, COND_ROWS, ADA_SIDE_ROWS), lambda r: (r, 0, 0))]
        in_specs += [part_spec(p) for p in range(ADA_SIDE_PARTS)]
        in_specs += [pl.BlockSpec((1, 1, N_ADA), lambda r: (layer + 1, 0, 0))]
        out_specs.append(pl.BlockSpec((COND_ROWS, N_ADA), lambda r: (0, 0)))
        out_shape.append(jax.ShapeDtypeStruct((COND_ROWS, N_ADA), F32))
        args += [cond_slabs] + [w_ada] * ADA_SIDE_PARTS + [b_ada.reshape(DEPTH, 1, N_ADA)]
    return pl.pallas_call(
        functools.partial(_attn_kernel, layer=layer, with_mod=with_mod),
        grid=(N_ATTN_STEPS,),
        in_specs=in_specs,
        out_specs=out_specs,
        out_shape=out_shape,
        compiler_params=_params("arbitrary"),
        name="swa_attention",
    )(*args)


def _merge_kernel(h_ref, yp_ref, ya_ref, wgp_ref, wga_ref, wbp_ref, wba_ref, o_ref,
                  sgp, sga, sbp, sba):
    j, i, slot = _ws_ids()

    def cast():
        _cast_chunk(wgp_ref, sgp, slot, i)
        _cast_chunk(wga_ref, sga, slot, i)
        _cast_chunk(wbp_ref, sbp, slot, i)
        _cast_chunk(wba_ref, sba, slot, i)

    @pl.when(j == 0)
    def _():
        cast()

    @pl.when(j > 0)
    def _():
        h = h_ref[...]
        gate_pool = jax.nn.sigmoid(_dot(h, sgp[1 - slot]))
        gate_attn = jax.nn.sigmoid(_dot(h, sga[1 - slot]))
        y_pool = _dot(yp_ref[...], sbp[1 - slot])
        y_attn = _dot(ya_ref[...], sba[1 - slot])
        o_ref[...] = (gate_pool * y_pool + gate_attn * y_attn).astype(o_ref.dtype)
        cast()


def _merge(h, yp, ya, w_gate, w_bp, w_ba, layer, tm=512, tn=512):
    ni, nj = ROWS // tm, D_MODEL // tn
    return pl.pallas_call(
        _merge_kernel,
        grid=(nj + 1, ni),
        in_specs=[
            _a_spec(tm, D_MODEL), _a_spec(tm, POOL_WIDTH), _a_spec(tm, Q_WIDTH),
            _w_spec(layer, D_MODEL, ni, tn, nj),
            _w_spec(layer, D_MODEL, ni, tn, nj, col_tile0=nj),
            _w_spec(layer, POOL_WIDTH, ni, tn, nj),
            _w_spec(layer, Q_WIDTH, ni, tn, nj),
        ],
        out_specs=_o_spec(tm, tn),
        out_shape=jax.ShapeDtypeStruct((ROWS, D_MODEL), BF16),
        scratch_shapes=[pltpu.VMEM((2, D_MODEL, tn), BF16), pltpu.VMEM((2, D_MODEL, tn), BF16),
                        pltpu.VMEM((2, POOL_WIDTH, tn), BF16), pltpu.VMEM((2, Q_WIDTH, tn), BF16)],
        compiler_params=_params("arbitrary", "arbitrary"),
        name="gated_merge",
    )(h, yp, ya, w_gate, w_gate, w_bp, w_ba)


def _resid_kernel(a_ref, w_ref, x_ref, g_ref, o_ref, scr):
    j, i, slot = _ws_ids()

    @pl.when(j == 0)
    def _():
        _cast_chunk(w_ref, scr, slot, i)

    @pl.when(j > 0)
    def _():
        o_ref[...] = x_ref[...] + g_ref[0] * _dot(a_ref[...], scr[1 - slot])
        _cast_chunk(w_ref, scr, slot, i)


def _resid_proj(a, w, x, gate, layer, tm, tn):
    m, k = a.shape
    n = x.shape[1]
    ni, nj = m // tm, n // tn
    per_batch = SEQ // tm
    return pl.pallas_call(
        _resid_kernel,
        grid=(nj + 1, ni),
        in_specs=[
            _a_spec(tm, k),
            _w_spec(layer, k, ni, tn, nj),
            _o_spec(tm, tn),
            pl.BlockSpec((1, 1, tn), lambda j, i: (i // per_batch, 0, jnp.maximum(j - 1, 0))),
        ],
        out_specs=_o_spec(tm, tn),
        out_shape=jax.ShapeDtypeStruct((m, n), F32),
        scratch_shapes=[pltpu.VMEM((2, k, tn), BF16)],
        compiler_params=_params("arbitrary", "arbitrary"),
        name="resid_proj",
    )(a, w, x, gate)


FF_TILE = V7X_MXU_DIM
SWIGLU_ROWS = 1024


def _swiglu_kernel(a_ref, wa_ref, wb_ref, o_ref, scr):
    j, i, slot = _ws_ids()

    def cast():
        _cast_chunk(wa_ref, scr, slot, i)
        _cast_chunk(wb_ref, scr, slot, i, col0=FF_TILE)

    @pl.when(j == 0)
    def _():
        cast()

    @pl.when(j > 0)
    def _():
        w = scr[1 - slot]
        for r0 in range(0, a_ref.shape[0], SWIGLU_ROWS):
            ab = _dot(a_ref[r0:r0 + SWIGLU_ROWS, :], w)
            a = ab[:, :FF_TILE]
            b = ab[:, FF_TILE:]
            o_ref[r0:r0 + SWIGLU_ROWS, :] = ((a * jax.nn.sigmoid(a)) * b).astype(o_ref.dtype)
        cast()


def _swiglu(h2, w_ffn_in, layer, tm=2048):
    ni, nj = ROWS // tm, D_FF // FF_TILE
    return pl.pallas_call(
        _swiglu_kernel,
        grid=(nj + 1, ni),
        in_specs=[
            _a_spec(tm, D_MODEL),
            _w_spec(layer, D_MODEL, ni, FF_TILE, nj),
            _w_spec(layer, D_MODEL, ni, FF_TILE, nj, col_tile0=nj),
        ],
        out_specs=_o_spec(tm, FF_TILE),
        out_shape=jax.ShapeDtypeStruct((ROWS, D_FF), BF16),
        scratch_shapes=[pltpu.VMEM((2, D_MODEL, 2 * FF_TILE), BF16)],
        compiler_params=_params("arbitrary", "arbitrary"),
        name="swiglu_up",
    )(h2, w_ffn_in, w_ffn_in)


def kernel(x, c, w_ada, b_ada, norm1, w_in, w_pool_mix, pool_scale, sinks, rel_bias,
           w_branch_pool, w_branch_attn, w_gate, w_out, norm2, w_ffn_in, w_ffn_out,
           final_norm):
    c_pad = jnp.pad(c, ((0, COND_ROWS - BATCH), (0, 0)))
    cond_slabs = c_pad.reshape(COND_ROWS, N_ATTN_STEPS, ADA_SIDE_ROWS).transpose(1, 0, 2)
    mod = _ada_first(c_pad, w_ada, b_ada)
    bias = _band_bias(rel_bias, _bucket_tables())
    xr = x.reshape(ROWS, D_MODEL)

    for l in range(DEPTH):
        sh1, sc1, g1, sh2, sc2, g2 = [m.reshape(BATCH, 1, D_MODEL)
                                      for m in jnp.split(mod[:BATCH], N_MOD, axis=-1)]
        h = _mod_norm(xr, norm1, l, sc1, sh1)
        u, q, kv = _proj(h, w_in, l)
        yp = _pool_branch(u, w_pool_mix, pool_scale, l)
        if l + 1 < DEPTH:
            ya, mod = _attention(q, kv, sinks, bias, l, cond_slabs, w_ada, b_ada)
        else:
            ya, = _attention(q, kv, sinks, bias, l)
        merged = _merge(h, yp, ya, w_gate, w_branch_pool, w_branch_attn, l)
        xr = _resid_proj(merged, w_out, xr, g1, l, tm=512, tn=1024)

        h2 = _mod_norm(xr, norm2, l, sc2, sh2)
        act = _swiglu(h2, w_ffn_in, l)
        xr = _resid_proj(act, w_ffn_out, xr, g2, l, tm=512, tn=512)

    return _final_norm(xr, final_norm).reshape(BATCH, SEQ, D_MODEL)
```

```python
import functools
import math

import jax
import jax.numpy as jnp
from jax import lax
from jax.experimental import pallas as pl
from jax.experimental.pallas import tpu as pltpu

D_MODEL = 4096
BATCH = 4
SEQ = 2048
ROWS = BATCH * SEQ
DEPTH = 2
POOL_WINDOWS = (2, 4, 8, 16)
N_GROUPS = len(POOL_WINDOWS)
POOL_WIDTH = D_MODEL // 2
POOL_GROUP = POOL_WIDTH // N_GROUPS
HEAD_DIM = 64
N_Q_HEADS = (D_MODEL // 2) // HEAD_DIM
N_KV_HEADS = N_Q_HEADS // 8
Q_PER_KV = N_Q_HEADS // N_KV_HEADS
Q_WIDTH = N_Q_HEADS * HEAD_DIM
KV_WIDTH = N_KV_HEADS * HEAD_DIM
WINDOW = 128
BLOCK = 128
N_BLOCKS = SEQ // BLOCK
N_BUCKETS = 32
MAX_DISTANCE = 128
IN_WIDTH = POOL_WIDTH + Q_WIDTH + 2 * KV_WIDTH
D_FF = -(-(8 * D_MODEL) // (3 * 256)) * 256
N_MOD = 6
EPS = 1e-6

V7X_VMEM_BYTES = 64 * 1024 * 1024
VMEM_LIMIT = V7X_VMEM_BYTES - 8 * 1024 * 1024
V7X_MXU_DIM = 256
V7X_SUBLANES = 8
V7X_BF16_SUBLANES = 16
COND_ROWS = V7X_SUBLANES

NEG = -1e30

BF16 = jnp.bfloat16
F32 = jnp.float32


def _params(*sem):
    return pltpu.CompilerParams(dimension_semantics=sem, vmem_limit_bytes=VMEM_LIMIT)


def _dot(a, b):
    return jnp.dot(a, b, preferred_element_type=F32)


ADA_ROWS = 128
N_ADA = N_MOD * D_MODEL


def _silu_bf16(c):
    return (c * jax.nn.sigmoid(c)).astype(BF16)


def _ada_kernel(c_ref, w_ref, b_ref, o_ref):
    k = pl.program_id(0)

    @pl.when(k == 0)
    def _():
        o_ref[...] = jnp.broadcast_to(b_ref[0], o_ref.shape)

    o_ref[...] += _dot(_silu_bf16(c_ref[...]), w_ref[0].astype(BF16))


def _ada_first(c_pad, w_ada, b_ada):
    return pl.pallas_call(
        _ada_kernel,
        grid=(D_MODEL // ADA_ROWS,),
        in_specs=[
            pl.BlockSpec((COND_ROWS, ADA_ROWS), lambda k: (0, k)),
            pl.BlockSpec((1, ADA_ROWS, N_ADA), lambda k: (0, k, 0)),
            pl.BlockSpec((1, 1, N_ADA), lambda k: (0, 0, 0)),
        ],
        out_specs=pl.BlockSpec((COND_ROWS, N_ADA), lambda k: (0, 0)),
        out_shape=jax.ShapeDtypeStruct((COND_ROWS, N_ADA), F32),
        compiler_params=_params("arbitrary"),
        name="ada_mod",
    )(c_pad, w_ada, b_ada.reshape(DEPTH, 1, N_ADA))


NORM_CHUNK = 64


NORM_RING = 3
NORM_TM = 512


def _norm_kernel(x_hbm, g_ref, sc_ref, sh_ref, o_ref, ring, sems):
    step = pl.program_id(0)
    n_steps = pl.num_programs(0)

    def tile_copy(t, slot):
        rows = pl.ds(pl.multiple_of(t * NORM_TM, NORM_TM), NORM_TM)
        return pltpu.make_async_copy(x_hbm.at[rows, :], ring.at[slot], sems.at[slot])

    @pl.when(step == 0)
    def _():
        for t in range(NORM_RING - 1):
            tile_copy(t, t).start()

    ahead = step + (NORM_RING - 1)

    @pl.when(ahead < n_steps)
    def _():
        tile_copy(ahead, ahead % NORM_RING).start()

    slot = step % NORM_RING
    tile_copy(step, slot).wait()

    gain = g_ref[0] * (1.0 + sc_ref[0])
    shift = sh_ref[0]

    def chunk(c, carry):
        rows = pl.ds(pl.multiple_of(c * NORM_CHUNK, NORM_CHUNK), NORM_CHUNK)
        x = ring[slot, rows, :]
        r = lax.rsqrt(jnp.mean(x * x, axis=-1, keepdims=True) + EPS)
        o_ref[rows, :] = ((x * r) * gain + shift).astype(o_ref.dtype)
        return carry

    lax.fori_loop(0, NORM_TM // NORM_CHUNK, chunk, 0)


def _mod_norm(x, gain, layer, scale, shift):
    per_batch = SEQ // NORM_TM
    row = lambda i: (i // per_batch, 0, 0)
    return pl.pallas_call(
        _norm_kernel,
        grid=(ROWS // NORM_TM,),
        in_specs=[
            pl.BlockSpec(memory_space=pl.ANY),
            pl.BlockSpec((1, 1, D_MODEL), lambda i: (layer, 0, 0)),
            pl.BlockSpec((1, 1, D_MODEL), row),
            pl.BlockSpec((1, 1, D_MODEL), row),
        ],
        out_specs=pl.BlockSpec((NORM_TM, D_MODEL), lambda i: (i, 0)),
        out_shape=jax.ShapeDtypeStruct((ROWS, D_MODEL), BF16),
        scratch_shapes=[pltpu.VMEM((NORM_RING, NORM_TM, D_MODEL), F32),
                        pltpu.SemaphoreType.DMA((NORM_RING,))],
        compiler_params=_params("arbitrary"),
        name="mod_norm",
    )(x, gain.reshape(DEPTH, 1, D_MODEL), scale, shift)


def _final_norm_kernel(x_ref, g_ref, o_ref):
    gain = g_ref[...]

    def chunk(c, carry):
        rows = pl.ds(pl.multiple_of(c * NORM_CHUNK, NORM_CHUNK), NORM_CHUNK)
        x = x_ref[rows, :]
        r = lax.rsqrt(jnp.mean(x * x, axis=-1, keepdims=True) + EPS)
        o_ref[rows, :] = x * r * gain
        return carry

    lax.fori_loop(0, x_ref.shape[0] // NORM_CHUNK, chunk, 0)


def _final_norm(x, gain):
    tm = 512
    return pl.pallas_call(
        _final_norm_kernel,
        grid=(ROWS // tm,),
        in_specs=[
            pl.BlockSpec((tm, D_MODEL), lambda i: (i, 0)),
            pl.BlockSpec((1, D_MODEL), lambda i: (0, 0)),
        ],
        out_specs=pl.BlockSpec((tm, D_MODEL), lambda i: (i, 0)),
        out_shape=jax.ShapeDtypeStruct((ROWS, D_MODEL), F32),
        compiler_params=_params("arbitrary"),
        name="final_norm",
    )(x, gain.reshape(1, D_MODEL))


def _cast_chunk(w_ref, scr, slot, step, col0=0):
    rows, cols = w_ref.shape[1], w_ref.shape[2]
    r0 = pl.multiple_of(step * rows, rows)
    scr[slot, pl.ds(r0, rows), col0:col0 + cols] = w_ref[0].astype(BF16)


def _ws_ids():
    j = pl.program_id(0)
    return j, pl.program_id(1), j % 2


def _w_spec(layer, k, ni, cols, n_tiles, col_tile0=0):
    return pl.BlockSpec((1, k // ni, cols),
                        lambda j, i: (layer, i, col_tile0 + jnp.minimum(j, n_tiles - 1)))


def _a_spec(tm, k, col_block=0):
    return pl.BlockSpec((tm, k), lambda j, i: (jnp.where(j > 0, i, 0), col_block))


def _o_spec(tm, tn):
    return pl.BlockSpec((tm, tn), lambda j, i: (jnp.where(j > 0, i, 0), jnp.maximum(j - 1, 0)))


PROJ_TN = 512
PROJ_U_TILES = POOL_WIDTH // PROJ_TN
PROJ_Q_TILES = Q_WIDTH // PROJ_TN
PROJ_KV_TILES = 2 * KV_WIDTH // PROJ_TN


def _proj_kernel(a_ref, w_ref, u_ref, q_ref, kv_ref, scr):
    j, i, slot = _ws_ids()
    q_start = 1 + PROJ_U_TILES
    kv_start = q_start + PROJ_Q_TILES

    @pl.when(j == 0)
    def _():
        _cast_chunk(w_ref, scr, slot, i)

    @pl.when((j >= 1) & (j < q_start))
    def _():
        u_ref[...] = _dot(a_ref[...], scr[1 - slot])
        _cast_chunk(w_ref, scr, slot, i)

    @pl.when((j >= q_start) & (j < kv_start))
    def _():
        q_ref[...] = (_dot(a_ref[...], scr[1 - slot]) * HEAD_DIM ** -0.5).astype(q_ref.dtype)
        _cast_chunk(w_ref, scr, slot, i)

    @pl.when(j >= kv_start)
    def _():
        kv_ref[...] = _dot(a_ref[...], scr[1 - slot]).astype(kv_ref.dtype)
        _cast_chunk(w_ref, scr, slot, i)


def _proj_out_spec(tm, ni, first, n_tiles):
    def index(j, i):
        row = jnp.where(j < first, 0, jnp.where(j < first + n_tiles, i, ni - 1))
        return row, jnp.clip(j - first, 0, n_tiles - 1)
    return pl.BlockSpec((tm, PROJ_TN), index)


def _proj(a, w, layer, tm=1024):
    m, k = a.shape
    ni, nj = m // tm, IN_WIDTH // PROJ_TN
    q_start = 1 + PROJ_U_TILES
    return pl.pallas_call(
        _proj_kernel,
        grid=(nj + 1, ni),
        in_specs=[_a_spec(tm, k), _w_spec(layer, k, ni, PROJ_TN, nj)],
        out_specs=[_proj_out_spec(tm, ni, 1, PROJ_U_TILES),
                   _proj_out_spec(tm, ni, q_start, PROJ_Q_TILES),
                   _proj_out_spec(tm, ni, q_start + PROJ_Q_TILES, PROJ_KV_TILES)],
        out_shape=[jax.ShapeDtypeStruct((m, POOL_WIDTH), F32),
                   jax.ShapeDtypeStruct((m, Q_WIDTH), BF16),
                   jax.ShapeDtypeStruct((m, 2 * KV_WIDTH), BF16)],
        scratch_shapes=[pltpu.VMEM((2, k, PROJ_TN), BF16)],
        compiler_params=_params("arbitrary", "arbitrary"),
        name="proj",
    )(a, w)


POOL_PAD = 16


def _pool_kernel(u_ref, w_ref, s_ref, o_ref, a_ref, b_ref):
    g = pl.program_id(1)
    zeros = jnp.zeros((POOL_PAD, POOL_GROUP), F32)
    a_ref[0:POOL_PAD, :] = zeros
    b_ref[0:POOL_PAD, :] = zeros
    body = pl.ds(POOL_PAD, SEQ)
    a_ref[body, :] = u_ref[...]
    row = lax.broadcasted_iota(jnp.int32, (SEQ, 1), 0)

    def lagged(ref, lag):
        return ref[pl.ds(POOL_PAD - lag, SEQ), :]

    for group in range(N_GROUPS):
        @pl.when(g == group)
        def _():
            src, dst = a_ref, b_ref
            for level in range(group):
                dst[body, :] = src[body, :] + lagged(src, 1 << level)
                src, dst = dst, src
            total = src[body, :] + lagged(src, 1 << group)
            inv_count = 1.0 / jnp.minimum(row + 1, POOL_WINDOWS[group]).astype(F32)
            pooled = total * inv_count - u_ref[...]
            y = _dot(pooled.astype(BF16), w_ref[0, 0].astype(BF16))
            o_ref[...] = (y * s_ref[0]).astype(o_ref.dtype)


def _pool_branch(u, w_mix, scale, layer):
    return pl.pallas_call(
        _pool_kernel,
        grid=(BATCH, N_GROUPS),
        in_specs=[
            pl.BlockSpec((SEQ, POOL_GROUP), lambda b, g: (b, g)),
            pl.BlockSpec((1, 1, POOL_GROUP, POOL_GROUP), lambda b, g: (layer, g, 0, 0)),
            pl.BlockSpec((1, 1, POOL_GROUP), lambda b, g: (layer, 0, g)),
        ],
        out_specs=pl.BlockSpec((SEQ, POOL_GROUP), lambda b, g: (b, g)),
        out_shape=jax.ShapeDtypeStruct((ROWS, POOL_WIDTH), BF16),
        scratch_shapes=[pltpu.VMEM((POOL_PAD + SEQ, POOL_GROUP), F32),
                        pltpu.VMEM((POOL_PAD + SEQ, POOL_GROUP), F32)],
        compiler_params=_params("arbitrary", "arbitrary"),
        name="pool_branch",
    )(u, w_mix, scale.reshape(DEPTH, 1, POOL_WIDTH))


BAND = 2 * BLOCK
PAIRS = Q_PER_KV // 2


def _bias_kernel(rel_ref, bucket_ref, o_ref):
    h = pl.program_id(1)
    bucket = bucket_ref[0]
    for p in range(PAIRS):
        for e in range(2):
            head = h * Q_PER_KV + 2 * p + e
            acc = jnp.full(bucket.shape, NEG, F32)
            for b in range(N_BUCKETS):
                acc = jnp.where(bucket == b, rel_ref[b, head], acc)
            r0 = ((p // 2) * 2 + e) * BAND
            c0 = (p % 2) * BLOCK
            o_ref[0, 0, r0:r0 + BAND, c0:c0 + BLOCK] = acc


def _band_bias(rel_bias, bucket_pair):
    return pl.pallas_call(
        _bias_kernel,
        grid=(2, N_KV_HEADS),
        in_specs=[
            pl.BlockSpec(memory_space=pltpu.SMEM),
            pl.BlockSpec((1, BAND, BLOCK), lambda f, h: (f, 0, 0)),
        ],
        out_specs=pl.BlockSpec((1, 1, PAIRS * BAND, 2 * BLOCK), lambda f, h: (f, h, 0, 0)),
        out_shape=jax.ShapeDtypeStruct((2, N_KV_HEADS, PAIRS * BAND, 2 * BLOCK), F32),
        compiler_params=_params("arbitrary", "arbitrary"),
        name="band_bias",
    )(rel_bias, bucket_pair)


def _bucket_tables():
    i = jnp.arange(BLOCK)[None, :]
    j = jnp.arange(BAND)[:, None]
    dist = i + BLOCK - j
    max_exact = N_BUCKETS // 2
    d = jnp.maximum(dist, 0)
    log_ratio = jnp.log(jnp.maximum(d, 1).astype(F32) / max_exact) / math.log(MAX_DISTANCE / max_exact)
    large = jnp.minimum(max_exact + (log_ratio * (N_BUCKETS - max_exact)).astype(jnp.int32), N_BUCKETS - 1)
    bucket = jnp.where(d < max_exact, d, large)
    in_window = (dist >= 0) & (dist < WINDOW)
    later = jnp.where(in_window, bucket, -1)
    first = jnp.where(j >= BLOCK, later, -1)
    return jnp.stack([first, later]).astype(jnp.int32)


N_ATTN_STEPS = ROWS // BLOCK
ADA_SIDE_ROWS = D_MODEL // N_ATTN_STEPS
N_ATTN_GROUPS = N_KV_HEADS * (PAIRS // 2)
ADA_SIDE_COLS = N_ADA // N_ATTN_GROUPS


def _attn_kernel(sink_ref, q_ref, kvp_ref, kvc_ref, bias_ref, *rest, layer, with_mod):
    if with_mod:
        c_ref, wada_ref, bada_ref, o_ref, mod_ref = rest

        @pl.when(pl.program_id(0) == 0)
        def _():
            mod_ref[...] = jnp.broadcast_to(bada_ref[0], mod_ref.shape)

        cond = _silu_bf16(c_ref[0])

        def next_mod(n):
            cols = slice(n * ADA_SIDE_COLS, (n + 1) * ADA_SIDE_COLS)
            mod_ref[:, cols] += _dot(cond, wada_ref[0, :, cols].astype(BF16))
    else:
        o_ref, = rest

        def next_mod(n):
            pass

    lane = lax.broadcasted_iota(jnp.int32, (1, 2 * HEAD_DIM), 1)
    low = lane < HEAD_DIM
    left = lax.broadcasted_iota(jnp.int32, (1, 2 * BLOCK), 1) < BLOCK
    ones = jnp.ones((V7X_BF16_SUBLANES, BAND), BF16)
    nt = (((1,), (1,)), ((), ()))
    keys, values = [], []
    for slab in range(N_KV_HEADS // 2):
        c = slab * 2 * HEAD_DIM
        k_slab = jnp.concatenate([kvp_ref[:, c:c + 2 * HEAD_DIM], kvc_ref[:, c:c + 2 * HEAD_DIM]], axis=0)
        v_slab = jnp.concatenate([kvp_ref[:, KV_WIDTH + c:KV_WIDTH + c + 2 * HEAD_DIM],
                                  kvc_ref[:, KV_WIDTH + c:KV_WIDTH + c + 2 * HEAD_DIM]], axis=0)
        k_swap = pltpu.roll(k_slab, HEAD_DIM, axis=1)
        v_t = v_slab.astype(F32).T.astype(BF16)
        zero = jnp.zeros_like(k_slab)
        for hh in range(2):
            k_on_low, k_on_high = (k_slab, k_swap) if hh == 0 else (k_swap, k_slab)
            keys.append((jnp.where(low, k_on_low, zero), jnp.where(low, zero, k_on_high)))
            values.append(jnp.concatenate([v_t[hh * HEAD_DIM:(hh + 1) * HEAD_DIM], ones], axis=0))

    def scores(h, t):
        c0 = h * Q_PER_KV * HEAD_DIM + t * 2 * BLOCK
        q2 = jnp.concatenate([q_ref[:, c0:c0 + BLOCK], q_ref[:, c0 + BLOCK:c0 + 2 * BLOCK]], axis=0)
        return [lax.dot_general(keys[h][e], q2, nt, preferred_element_type=F32)
                + bias_ref[0, h, (t * 2 + e) * BAND:(t * 2 + e + 1) * BAND, :] for e in range(2)]

    def softmax(h, t, s_pair):
        probs, tails = [], []
        for e in range(2):
            head0 = h * Q_PER_KV + 4 * t + e
            sink = jnp.where(left, sink_ref[layer, head0], sink_ref[layer, head0 + 2])
            m = jnp.maximum(jnp.max(s_pair[e], axis=0, keepdims=True), sink)
            probs.append(jnp.exp(s_pair[e] - m).astype(BF16))
            tails.append(jnp.exp(sink - m))
        return probs, tails

    def finish(h, t, probs, tails):
        for pp in range(2):
            cols = slice(pp * BLOCK, (pp + 1) * BLOCK)
            p_t = jnp.concatenate([probs[0][:, cols], probs[1][:, cols]], axis=1)
            tail = jnp.concatenate([tails[0][:, cols], tails[1][:, cols]], axis=1)
            res = _dot(values[h], p_t)
            out_t = res[:HEAD_DIM] * (1.0 / (res[HEAD_DIM:HEAD_DIM + 1] + tail))
            pair_t = jnp.concatenate([out_t[:, :BLOCK], out_t[:, BLOCK:]], axis=0)
            o0 = h * Q_PER_KV * HEAD_DIM + (2 * t + pp) * BLOCK
            o_ref[:, o0:o0 + BLOCK] = pair_t.T.astype(o_ref.dtype)

    groups = [(h, t) for h in range(N_KV_HEADS) for t in range(PAIRS // 2)]
    pending = scores(*groups[0])
    ready = None
    for n, group in enumerate(groups):
        current = pending
        if n + 1 < len(groups):
            pending = scores(*groups[n + 1])
        weights = softmax(*group, current)
        next_mod(n)
        if ready is not None:
            finish(*groups[n - 1], *ready)
        ready = weights
    finish(*groups[-1], *ready)


def _attention(q, kv, sinks, bias, layer, cond_slabs=None, w_ada=None, b_ada=None):
    with_mod = cond_slabs is not None
    in_specs = [
        pl.BlockSpec(memory_space=pltpu.SMEM),
        pl.BlockSpec((BLOCK, Q_WIDTH), lambda r: (r, 0)),
        pl.BlockSpec((BLOCK, 2 * KV_WIDTH), lambda r: (jnp.maximum(r - 1, 0), 0)),
        pl.BlockSpec((BLOCK, 2 * KV_WIDTH), lambda r: (r, 0)),
        pl.BlockSpec((1, N_KV_HEADS, PAIRS * BAND, 2 * BLOCK),
                     lambda r: (jnp.minimum(r % N_BLOCKS, 1), 0, 0, 0)),
    ]
    out_specs = [pl.BlockSpec((BLOCK, Q_WIDTH), lambda r: (r, 0))]
    out_shape = [jax.ShapeDtypeStruct((ROWS, Q_WIDTH), BF16)]
    args = [sinks, q, kv, kv, bias]
    if with_mod:
        in_specs += [
            pl.BlockSpec((1, COND_ROWS, ADA_SIDE_ROWS), lambda r: (r, 0, 0)),
            pl.BlockSpec((1, ADA_SIDE_ROWS, N_ADA), lambda r: (layer + 1, r, 0)),
            pl.BlockSpec((1, 1, N_ADA), lambda r: (layer + 1, 0, 0)),
        ]
        out_specs.append(pl.BlockSpec((COND_ROWS, N_ADA), lambda r: (0, 0)))
        out_shape.append(jax.ShapeDtypeStruct((COND_ROWS, N_ADA), F32))
        args += [cond_slabs, w_ada, b_ada.reshape(DEPTH, 1, N_ADA)]
    return pl.pallas_call(
        functools.partial(_attn_kernel, layer=layer, with_mod=with_mod),
        grid=(N_ATTN_STEPS,),
        in_specs=in_specs,
        out_specs=out_specs,
        out_shape=out_shape,
        compiler_params=_params("arbitrary"),
        name="swa_attention",
    )(*args)


def _merge_kernel(h_ref, yp_ref, ya_ref, wgp_ref, wga_ref, wbp_ref, wba_ref, o_ref,
                  sgp, sga, sbp, sba):
    j, i, slot = _ws_ids()

    def cast():
        _cast_chunk(wgp_ref, sgp, slot, i)
        _cast_chunk(wga_ref, sga, slot, i)
        _cast_chunk(wbp_ref, sbp, slot, i)
        _cast_chunk(wba_ref, sba, slot, i)

    @pl.when(j == 0)
    def _():
        cast()

    @pl.when(j > 0)
    def _():
        h = h_ref[...]
        gate_pool = jax.nn.sigmoid(_dot(h, sgp[1 - slot]))
        gate_attn = jax.nn.sigmoid(_dot(h, sga[1 - slot]))
        y_pool = _dot(yp_ref[...], sbp[1 - slot])
        y_attn = _dot(ya_ref[...], sba[1 - slot])
        o_ref[...] = (gate_pool * y_pool + gate_attn * y_attn).astype(o_ref.dtype)
        cast()


def _merge(h, yp, ya, w_gate, w_bp, w_ba, layer, tm=512, tn=512):
    ni, nj = ROWS // tm, D_MODEL // tn
    return pl.pallas_call(
        _merge_kernel,
        grid=(nj + 1, ni),
        in_specs=[
            _a_spec(tm, D_MODEL), _a_spec(tm, POOL_WIDTH), _a_spec(tm, Q_WIDTH),
            _w_spec(layer, D_MODEL, ni, tn, nj),
            _w_spec(layer, D_MODEL, ni, tn, nj, col_tile0=nj),
            _w_spec(layer, POOL_WIDTH, ni, tn, nj),
            _w_spec(layer, Q_WIDTH, ni, tn, nj),
        ],
        out_specs=_o_spec(tm, tn),
        out_shape=jax.ShapeDtypeStruct((ROWS, D_MODEL), BF16),
        scratch_shapes=[pltpu.VMEM((2, D_MODEL, tn), BF16), pltpu.VMEM((2, D_MODEL, tn), BF16),
                        pltpu.VMEM((2, POOL_WIDTH, tn), BF16), pltpu.VMEM((2, Q_WIDTH, tn), BF16)],
        compiler_params=_params("arbitrary", "arbitrary"),
        name="gated_merge",
    )(h, yp, ya, w_gate, w_gate, w_bp, w_ba)


def _resid_kernel(a_ref, w_ref, x_ref, g_ref, o_ref, scr):
    j, i, slot = _ws_ids()

    @pl.when(j == 0)
    def _():
        _cast_chunk(w_ref, scr, slot, i)

    @pl.when(j > 0)
    def _():
        o_ref[...] = x_ref[...] + g_ref[0] * _dot(a_ref[...], scr[1 - slot])
        _cast_chunk(w_ref, scr, slot, i)


def _resid_proj(a, w, x, gate, layer, tm, tn):
    m, k = a.shape
    n = x.shape[1]
    ni, nj = m // tm, n // tn
    per_batch = SEQ // tm
    return pl.pallas_call(
        _resid_kernel,
        grid=(nj + 1, ni),
        in_specs=[
            _a_spec(tm, k),
            _w_spec(layer, k, ni, tn, nj),
            _o_spec(tm, tn),
            pl.BlockSpec((1, 1, tn), lambda j, i: (i // per_batch, 0, jnp.maximum(j - 1, 0))),
        ],
        out_specs=_o_spec(tm, tn),
        out_shape=jax.ShapeDtypeStruct((m, n), F32),
        scratch_shapes=[pltpu.VMEM((2, k, tn), BF16)],
        compiler_params=_params("arbitrary", "arbitrary"),
        name="resid_proj",
    )(a, w, x, gate)


FF_TILE = V7X_MXU_DIM
SWIGLU_ROWS = 1024


def _swiglu_kernel(a_ref, wa_ref, wb_ref, o_ref, scr):
    j, i, slot = _ws_ids()

    def cast():
        _cast_chunk(wa_ref, scr, slot, i)
        _cast_chunk(wb_ref, scr, slot, i, col0=FF_TILE)

    @pl.when(j == 0)
    def _():
        cast()

    @pl.when(j > 0)
    def _():
        w = scr[1 - slot]
        for r0 in range(0, a_ref.shape[0], SWIGLU_ROWS):
            ab = _dot(a_ref[r0:r0 + SWIGLU_ROWS, :], w)
            a = ab[:, :FF_TILE]
            b = ab[:, FF_TILE:]
            o_ref[r0:r0 + SWIGLU_ROWS, :] = ((a * jax.nn.sigmoid(a)) * b).astype(o_ref.dtype)
        cast()


def _swiglu(h2, w_ffn_in, layer, tm=2048):
    ni, nj = ROWS // tm, D_FF // FF_TILE
    return pl.pallas_call(
        _swiglu_kernel,
        grid=(nj + 1, ni),
        in_specs=[
            _a_spec(tm, D_MODEL),
            _w_spec(layer, D_MODEL, ni, FF_TILE, nj),
            _w_spec(layer, D_MODEL, ni, FF_TILE, nj, col_tile0=nj),
        ],
        out_specs=_o_spec(tm, FF_TILE),
        out_shape=jax.ShapeDtypeStruct((ROWS, D_FF), BF16),
        scratch_shapes=[pltpu.VMEM((2, D_MODEL, 2 * FF_TILE), BF16)],
        compiler_params=_params("arbitrary", "arbitrary"),
        name="swiglu_up",
    )(h2, w_ffn_in, w_ffn_in)


def kernel(x, c, w_ada, b_ada, norm1, w_in, w_pool_mix, pool_scale, sinks, rel_bias,
           w_branch_pool, w_branch_attn, w_gate, w_out, norm2, w_ffn_in, w_ffn_out,
           final_norm):
    c_pad = jnp.pad(c, ((0, COND_ROWS - BATCH), (0, 0)))
    cond_slabs = c_pad.reshape(COND_ROWS, N_ATTN_STEPS, ADA_SIDE_ROWS).transpose(1, 0, 2)
    mod = _ada_first(c_pad, w_ada, b_ada)
    bias = _band_bias(rel_bias, _bucket_tables())
    xr = x.reshape(ROWS, D_MODEL)

    for l in range(DEPTH):
        sh1, sc1, g1, sh2, sc2, g2 = [m.reshape(BATCH, 1, D_MODEL)
                                      for m in jnp.split(mod[:BATCH], N_MOD, axis=-1)]
        h = _mod_norm(xr, norm1, l, sc1, sh1)
        u, q, kv = _proj(h, w_in, l)
        yp = _pool_branch(u, w_pool_mix, pool_scale, l)
        if l + 1 < DEPTH:
            ya, mod = _attention(q, kv, sinks, bias, l, cond_slabs, w_ada, b_ada)
        else:
            ya, = _attention(q, kv, sinks, bias, l)
        merged = _merge(h, yp, ya, w_gate, w_branch_pool, w_branch_attn, l)
        xr = _resid_proj(merged, w_out, xr, g1, l, tm=512, tn=1024)

        h2 = _mod_norm(xr, norm2, l, sc2, sh2)
        act = _swiglu(h2, w_ffn_in, l)
        xr = _resid_proj(act, w_ffn_out, xr, g2, l, tm=512, tn=512)

    return _final_norm(xr, final_norm).reshape(BATCH, SEQ, D_MODEL)
```

```python
import functools
import math

import jax
import jax.numpy as jnp
from jax import lax
from jax.experimental import pallas as pl
from jax.experimental.pallas import tpu as pltpu

D_MODEL = 4096
BATCH = 4
SEQ = 2048
ROWS = BATCH * SEQ
DEPTH = 2
POOL_WINDOWS = (2, 4, 8, 16)
N_GROUPS = len(POOL_WINDOWS)
POOL_WIDTH = D_MODEL // 2
POOL_GROUP = POOL_WIDTH // N_GROUPS
HEAD_DIM = 64
N_Q_HEADS = (D_MODEL // 2) // HEAD_DIM
N_KV_HEADS = N_Q_HEADS // 8
Q_PER_KV = N_Q_HEADS // N_KV_HEADS
Q_WIDTH = N_Q_HEADS * HEAD_DIM
KV_WIDTH = N_KV_HEADS * HEAD_DIM
WINDOW = 128
BLOCK = 128
N_BLOCKS = SEQ // BLOCK
N_BUCKETS = 32
MAX_DISTANCE = 128
IN_WIDTH = POOL_WIDTH + Q_WIDTH + 2 * KV_WIDTH
D_FF = -(-(8 * D_MODEL) // (3 * 256)) * 256
N_MOD = 6
EPS = 1e-6

V7X_VMEM_BYTES = 64 * 1024 * 1024
VMEM_LIMIT = V7X_VMEM_BYTES - 8 * 1024 * 1024
V7X_MXU_DIM = 256
V7X_SUBLANES = 8
V7X_BF16_SUBLANES = 16
COND_ROWS = V7X_SUBLANES

NEG = -1e30

BF16 = jnp.bfloat16
F32 = jnp.float32


def _params(*sem):
    return pltpu.CompilerParams(dimension_semantics=sem, vmem_limit_bytes=VMEM_LIMIT)


def _dot(a, b):
    return jnp.dot(a, b, preferred_element_type=F32)


ADA_ROWS = 128
N_ADA = N_MOD * D_MODEL


def _silu_bf16(c):
    return (c * jax.nn.sigmoid(c)).astype(BF16)


def _ada_kernel(c_ref, w_ref, b_ref, o_ref):
    k = pl.program_id(0)

    @pl.when(k == 0)
    def _():
        o_ref[...] = jnp.broadcast_to(b_ref[0], o_ref.shape)

    o_ref[...] += _dot(_silu_bf16(c_ref[...]), w_ref[0].astype(BF16))


def _ada_first(c_pad, w_ada, b_ada):
    return pl.pallas_call(
        _ada_kernel,
        grid=(D_MODEL // ADA_ROWS,),
        in_specs=[
            pl.BlockSpec((COND_ROWS, ADA_ROWS), lambda k: (0, k)),
            pl.BlockSpec((1, ADA_ROWS, N_ADA), lambda k: (0, k, 0)),
            pl.BlockSpec((1, 1, N_ADA), lambda k: (0, 0, 0)),
        ],
        out_specs=pl.BlockSpec((COND_ROWS, N_ADA), lambda k: (0, 0)),
        out_shape=jax.ShapeDtypeStruct((COND_ROWS, N_ADA), F32),
        compiler_params=_params("arbitrary"),
        name="ada_mod",
    )(c_pad, w_ada, b_ada.reshape(DEPTH, 1, N_ADA))


NORM_CHUNK = 64


NORM_RING = 3
NORM_TM = 512


def _ring_fetch(x_hbm, ring, sems):
    step = pl.program_id(0)
    n_steps = pl.num_programs(0)

    def tile_copy(t, slot):
        rows = pl.ds(pl.multiple_of(t * NORM_TM, NORM_TM), NORM_TM)
        return pltpu.make_async_copy(x_hbm.at[rows, :], ring.at[slot], sems.at[slot])

    @pl.when(step == 0)
    def _():
        for t in range(NORM_RING - 1):
            tile_copy(t, t).start()

    ahead = step + (NORM_RING - 1)

    @pl.when(ahead < n_steps)
    def _():
        tile_copy(ahead, ahead % NORM_RING).start()

    slot = step % NORM_RING
    tile_copy(step, slot).wait()
    return slot


def _norm_kernel(x_hbm, g_ref, sc_ref, sh_ref, o_ref, ring, sems):
    slot = _ring_fetch(x_hbm, ring, sems)
    gain = g_ref[0] * (1.0 + sc_ref[0])
    shift = sh_ref[0]

    def chunk(c, carry):
        rows = pl.ds(pl.multiple_of(c * NORM_CHUNK, NORM_CHUNK), NORM_CHUNK)
        x = ring[slot, rows, :]
        r = lax.rsqrt(jnp.mean(x * x, axis=-1, keepdims=True) + EPS)
        o_ref[rows, :] = ((x * r) * gain + shift).astype(o_ref.dtype)
        return carry

    lax.fori_loop(0, NORM_TM // NORM_CHUNK, chunk, 0)


def _mod_norm(x, gain, layer, scale, shift):
    per_batch = SEQ // NORM_TM
    row = lambda i: (i // per_batch, 0, 0)
    return pl.pallas_call(
        _norm_kernel,
        grid=(ROWS // NORM_TM,),
        in_specs=[
            pl.BlockSpec(memory_space=pl.ANY),
            pl.BlockSpec((1, 1, D_MODEL), lambda i: (layer, 0, 0)),
            pl.BlockSpec((1, 1, D_MODEL), row),
            pl.BlockSpec((1, 1, D_MODEL), row),
        ],
        out_specs=pl.BlockSpec((NORM_TM, D_MODEL), lambda i: (i, 0)),
        out_shape=jax.ShapeDtypeStruct((ROWS, D_MODEL), BF16),
        scratch_shapes=[pltpu.VMEM((NORM_RING, NORM_TM, D_MODEL), F32),
                        pltpu.SemaphoreType.DMA((NORM_RING,))],
        compiler_params=_params("arbitrary"),
        name="mod_norm",
    )(x, gain.reshape(DEPTH, 1, D_MODEL), scale, shift)


def _final_norm_kernel(x_hbm, g_ref, o_ref, ring, sems):
    slot = _ring_fetch(x_hbm, ring, sems)
    gain = g_ref[...]

    def chunk(c, carry):
        rows = pl.ds(pl.multiple_of(c * NORM_CHUNK, NORM_CHUNK), NORM_CHUNK)
        x = ring[slot, rows, :]
        r = lax.rsqrt(jnp.mean(x * x, axis=-1, keepdims=True) + EPS)
        o_ref[rows, :] = x * r * gain
        return carry

    lax.fori_loop(0, NORM_TM // NORM_CHUNK, chunk, 0)


def _final_norm(x, gain):
    return pl.pallas_call(
        _final_norm_kernel,
        grid=(ROWS // NORM_TM,),
        in_specs=[
            pl.BlockSpec(memory_space=pl.ANY),
            pl.BlockSpec((1, D_MODEL), lambda i: (0, 0)),
        ],
        out_specs=pl.BlockSpec((NORM_TM, D_MODEL), lambda i: (i, 0)),
        out_shape=jax.ShapeDtypeStruct((ROWS, D_MODEL), F32),
        scratch_shapes=[pltpu.VMEM((NORM_RING, NORM_TM, D_MODEL), F32),
                        pltpu.SemaphoreType.DMA((NORM_RING,))],
        compiler_params=_params("arbitrary"),
        name="final_norm",
    )(x, gain.reshape(1, D_MODEL))


def _cast_chunk(w_ref, scr, slot, step, col0=0):
    rows, cols = w_ref.shape[1], w_ref.shape[2]
    r0 = pl.multiple_of(step * rows, rows)
    scr[slot, pl.ds(r0, rows), col0:col0 + cols] = w_ref[0].astype(BF16)


def _ws_ids():
    j = pl.program_id(0)
    return j, pl.program_id(1), j % 2


def _w_spec(layer, k, ni, cols, n_tiles, col_tile0=0):
    return pl.BlockSpec((1, k // ni, cols),
                        lambda j, i: (layer, i, col_tile0 + jnp.minimum(j, n_tiles - 1)))


def _a_spec(tm, k, col_block=0):
    return pl.BlockSpec((tm, k), lambda j, i: (jnp.where(j > 0, i, 0), col_block))


def _o_spec(tm, tn):
    return pl.BlockSpec((tm, tn), lambda j, i: (jnp.where(j > 0, i, 0), jnp.maximum(j - 1, 0)))


PROJ_TN = 512
PROJ_U_TILES = POOL_WIDTH // PROJ_TN
PROJ_Q_TILES = Q_WIDTH // PROJ_TN
PROJ_KV_TILES = 2 * KV_WIDTH // PROJ_TN


def _proj_kernel(a_ref, w_ref, u_ref, q_ref, kv_ref, scr):
    j, i, slot = _ws_ids()
    q_start = 1 + PROJ_U_TILES
    kv_start = q_start + PROJ_Q_TILES

    @pl.when(j == 0)
    def _():
        _cast_chunk(w_ref, scr, slot, i)

    @pl.when((j >= 1) & (j < q_start))
    def _():
        u_ref[...] = _dot(a_ref[...], scr[1 - slot])
        _cast_chunk(w_ref, scr, slot, i)

    @pl.when((j >= q_start) & (j < kv_start))
    def _():
        q_ref[...] = (_dot(a_ref[...], scr[1 - slot]) * HEAD_DIM ** -0.5).astype(q_ref.dtype)
        _cast_chunk(w_ref, scr, slot, i)

    @pl.when(j >= kv_start)
    def _():
        kv_ref[...] = _dot(a_ref[...], scr[1 - slot]).astype(kv_ref.dtype)
        _cast_chunk(w_ref, scr, slot, i)


def _proj_out_spec(tm, ni, first, n_tiles):
    def index(j, i):
        row = jnp.where(j < first, 0, jnp.where(j < first + n_tiles, i, ni - 1))
        return row, jnp.clip(j - first, 0, n_tiles - 1)
    return pl.BlockSpec((tm, PROJ_TN), index)


def _proj(a, w, layer, tm=1024):
    m, k = a.shape
    ni, nj = m // tm, IN_WIDTH // PROJ_TN
    q_start = 1 + PROJ_U_TILES
    return pl.pallas_call(
        _proj_kernel,
        grid=(nj + 1, ni),
        in_specs=[_a_spec(tm, k), _w_spec(layer, k, ni, PROJ_TN, nj)],
        out_specs=[_proj_out_spec(tm, ni, 1, PROJ_U_TILES),
                   _proj_out_spec(tm, ni, q_start, PROJ_Q_TILES),
                   _proj_out_spec(tm, ni, q_start + PROJ_Q_TILES, PROJ_KV_TILES)],
        out_shape=[jax.ShapeDtypeStruct((m, POOL_WIDTH), F32),
                   jax.ShapeDtypeStruct((m, Q_WIDTH), BF16),
                   jax.ShapeDtypeStruct((m, 2 * KV_WIDTH), BF16)],
        scratch_shapes=[pltpu.VMEM((2, k, PROJ_TN), BF16)],
        compiler_params=_params("arbitrary", "arbitrary"),
        name="proj",
    )(a, w)


POOL_PAD = 16


def _pool_kernel(u_ref, w_ref, s_ref, o_ref, a_ref, b_ref):
    g = pl.program_id(1)
    zeros = jnp.zeros((POOL_PAD, POOL_GROUP), F32)
    a_ref[0:POOL_PAD, :] = zeros
    b_ref[0:POOL_PAD, :] = zeros
    body = pl.ds(POOL_PAD, SEQ)
    a_ref[body, :] = u_ref[...]
    row = lax.broadcasted_iota(jnp.int32, (SEQ, 1), 0)

    def lagged(ref, lag):
        return ref[pl.ds(POOL_PAD - lag, SEQ), :]

    for group in range(N_GROUPS):
        @pl.when(g == group)
        def _():
            src, dst = a_ref, b_ref
            for level in range(group):
                dst[body, :] = src[body, :] + lagged(src, 1 << level)
                src, dst = dst, src
            total = src[body, :] + lagged(src, 1 << group)
            inv_count = 1.0 / jnp.minimum(row + 1, POOL_WINDOWS[group]).astype(F32)
            pooled = total * inv_count - u_ref[...]
            y = _dot(pooled.astype(BF16), w_ref[0, 0].astype(BF16))
            o_ref[...] = (y * s_ref[0]).astype(o_ref.dtype)


def _pool_branch(u, w_mix, scale, layer):
    return pl.pallas_call(
        _pool_kernel,
        grid=(BATCH, N_GROUPS),
        in_specs=[
            pl.BlockSpec((SEQ, POOL_GROUP), lambda b, g: (b, g)),
            pl.BlockSpec((1, 1, POOL_GROUP, POOL_GROUP), lambda b, g: (layer, g, 0, 0)),
            pl.BlockSpec((1, 1, POOL_GROUP), lambda b, g: (layer, 0, g)),
        ],
        out_specs=pl.BlockSpec((SEQ, POOL_GROUP), lambda b, g: (b, g)),
        out_shape=jax.ShapeDtypeStruct((ROWS, POOL_WIDTH), BF16),
        scratch_shapes=[pltpu.VMEM((POOL_PAD + SEQ, POOL_GROUP), F32),
                        pltpu.VMEM((POOL_PAD + SEQ, POOL_GROUP), F32)],
        compiler_params=_params("arbitrary", "arbitrary"),
        name="pool_branch",
    )(u, w_mix, scale.reshape(DEPTH, 1, POOL_WIDTH))


BAND = 2 * BLOCK
PAIRS = Q_PER_KV // 2


def _bias_kernel(rel_ref, bucket_ref, o_ref):
    h = pl.program_id(1)
    bucket = bucket_ref[0]
    for p in range(PAIRS):
        for e in range(2):
            head = h * Q_PER_KV + 2 * p + e
            acc = jnp.full(bucket.shape, NEG, F32)
            for b in range(N_BUCKETS):
                acc = jnp.where(bucket == b, rel_ref[b, head], acc)
            r0 = ((p // 2) * 2 + e) * BAND
            c0 = (p % 2) * BLOCK
            o_ref[0, 0, r0:r0 + BAND, c0:c0 + BLOCK] = acc


def _band_bias(rel_bias, bucket_pair):
    return pl.pallas_call(
        _bias_kernel,
        grid=(2, N_KV_HEADS),
        in_specs=[
            pl.BlockSpec(memory_space=pltpu.SMEM),
            pl.BlockSpec((1, BAND, BLOCK), lambda f, h: (f, 0, 0)),
        ],
        out_specs=pl.BlockSpec((1, 1, PAIRS * BAND, 2 * BLOCK), lambda f, h: (f, h, 0, 0)),
        out_shape=jax.ShapeDtypeStruct((2, N_KV_HEADS, PAIRS * BAND, 2 * BLOCK), F32),
        compiler_params=_params("arbitrary", "arbitrary"),
        name="band_bias",
    )(rel_bias, bucket_pair)


def _bucket_tables():
    i = jnp.arange(BLOCK)[None, :]
    j = jnp.arange(BAND)[:, None]
    dist = i + BLOCK - j
    max_exact = N_BUCKETS // 2
    d = jnp.maximum(dist, 0)
    log_ratio = jnp.log(jnp.maximum(d, 1).astype(F32) / max_exact) / math.log(MAX_DISTANCE / max_exact)
    large = jnp.minimum(max_exact + (log_ratio * (N_BUCKETS - max_exact)).astype(jnp.int32), N_BUCKETS - 1)
    bucket = jnp.where(d < max_exact, d, large)
    in_window = (dist >= 0) & (dist < WINDOW)
    later = jnp.where(in_window, bucket, -1)
    first = jnp.where(j >= BLOCK, later, -1)
    return jnp.stack([first, later]).astype(jnp.int32)


N_ATTN_STEPS = ROWS // BLOCK
ADA_SIDE_ROWS = D_MODEL // N_ATTN_STEPS
N_ATTN_GROUPS = N_KV_HEADS * (PAIRS // 2)
ADA_SIDE_COLS = N_ADA // N_ATTN_GROUPS


def _attn_kernel(sink_ref, q_ref, kvp_ref, kvc_ref, bias_ref, *rest, layer, with_mod):
    if with_mod:
        c_ref, wada_ref, bada_ref, o_ref, mod_ref = rest

        @pl.when(pl.program_id(0) == 0)
        def _():
            mod_ref[...] = jnp.broadcast_to(bada_ref[0], mod_ref.shape)

        cond = _silu_bf16(c_ref[0])

        def next_mod(n):
            cols = slice(n * ADA_SIDE_COLS, (n + 1) * ADA_SIDE_COLS)
            mod_ref[:, cols] += _dot(cond, wada_ref[0, :, cols].astype(BF16))
    else:
        o_ref, = rest

        def next_mod(n):
            pass

    lane = lax.broadcasted_iota(jnp.int32, (1, 2 * HEAD_DIM), 1)
    low = lane < HEAD_DIM
    left = lax.broadcasted_iota(jnp.int32, (1, 2 * BLOCK), 1) < BLOCK
    ones = jnp.ones((V7X_BF16_SUBLANES, BAND), BF16)
    nt = (((1,), (1,)), ((), ()))
    keys, values = [], []
    for slab in range(N_KV_HEADS // 2):
        c = slab * 2 * HEAD_DIM
        k_slab = jnp.concatenate([kvp_ref[:, c:c + 2 * HEAD_DIM], kvc_ref[:, c:c + 2 * HEAD_DIM]], axis=0)
        v_slab = jnp.concatenate([kvp_ref[:, KV_WIDTH + c:KV_WIDTH + c + 2 * HEAD_DIM],
                                  kvc_ref[:, KV_WIDTH + c:KV_WIDTH + c + 2 * HEAD_DIM]], axis=0)
        k_swap = pltpu.roll(k_slab, HEAD_DIM, axis=1)
        v_t = v_slab.astype(F32).T.astype(BF16)
        zero = jnp.zeros_like(k_slab)
        for hh in range(2):
            k_on_low, k_on_high = (k_slab, k_swap) if hh == 0 else (k_swap, k_slab)
            keys.append((jnp.where(low, k_on_low, zero), jnp.where(low, zero, k_on_high)))
            values.append(jnp.concatenate([v_t[hh * HEAD_DIM:(hh + 1) * HEAD_DIM], ones], axis=0))

    def scores(h, t):
        c0 = h * Q_PER_KV * HEAD_DIM + t * 2 * BLOCK
        q2 = jnp.concatenate([q_ref[:, c0:c0 + BLOCK], q_ref[:, c0 + BLOCK:c0 + 2 * BLOCK]], axis=0)
        return [lax.dot_general(keys[h][e], q2, nt, preferred_element_type=F32)
                + bias_ref[0, h, (t * 2 + e) * BAND:(t * 2 + e + 1) * BAND, :] for e in range(2)]

    def softmax(h, t, s_pair):
        probs, tails = [], []
        for e in range(2):
            head0 = h * Q_PER_KV + 4 * t + e
            sink = jnp.where(left, sink_ref[layer, head0], sink_ref[layer, head0 + 2])
            m = jnp.maximum(jnp.max(s_pair[e], axis=0, keepdims=True), sink)
            probs.append(jnp.exp(s_pair[e] - m).astype(BF16))
            tails.append(jnp.exp(sink - m))
        return probs, tails

    def finish(h, t, probs, tails):
        for pp in range(2):
            cols = slice(pp * BLOCK, (pp + 1) * BLOCK)
            p_t = jnp.concatenate([probs[0][:, cols], probs[1][:, cols]], axis=1)
            tail = jnp.concatenate([tails[0][:, cols], tails[1][:, cols]], axis=1)
            res = _dot(values[h], p_t)
            out_t = res[:HEAD_DIM] * (1.0 / (res[HEAD_DIM:HEAD_DIM + 1] + tail))
            pair_t = jnp.concatenate([out_t[:, :BLOCK], out_t[:, BLOCK:]], axis=0)
            o0 = h * Q_PER_KV * HEAD_DIM + (2 * t + pp) * BLOCK
            o_ref[:, o0:o0 + BLOCK] = pair_t.T.astype(o_ref.dtype)

    groups = [(h, t) for h in range(N_KV_HEADS) for t in range(PAIRS // 2)]
    pending = scores(*groups[0])
    ready = None
    for n, group in enumerate(groups):
        current = pending
        if n + 1 < len(groups):
            pending = scores(*groups[n + 1])
        weights = softmax(*group, current)
        next_mod(n)
        if ready is not None:
            finish(*groups[n - 1], *ready)
        ready = weights
    finish(*groups[-1], *ready)


def _attention(q, kv, sinks, bias, layer, cond_slabs=None, w_ada=None, b_ada=None):
    with_mod = cond_slabs is not None
    in_specs = [
        pl.BlockSpec(memory_space=pltpu.SMEM),
        pl.BlockSpec((BLOCK, Q_WIDTH), lambda r: (r, 0)),
        pl.BlockSpec((BLOCK, 2 * KV_WIDTH), lambda r: (jnp.maximum(r - 1, 0), 0)),
        pl.BlockSpec((BLOCK, 2 * KV_WIDTH), lambda r: (r, 0)),
        pl.BlockSpec((1, N_KV_HEADS, PAIRS * BAND, 2 * BLOCK),
                     lambda r: (jnp.minimum(r % N_BLOCKS, 1), 0, 0, 0)),
    ]
    out_specs = [pl.BlockSpec((BLOCK, Q_WIDTH), lambda r: (r, 0))]
    out_shape = [jax.ShapeDtypeStruct((ROWS, Q_WIDTH), BF16)]
    args = [sinks, q, kv, kv, bias]
    if with_mod:
        in_specs += [
            pl.BlockSpec((1, COND_ROWS, ADA_SIDE_ROWS), lambda r: (r, 0, 0)),
            pl.BlockSpec((1, ADA_SIDE_ROWS, N_ADA), lambda r: (layer + 1, r, 0)),
            pl.BlockSpec((1, 1, N_ADA), lambda r: (layer + 1, 0, 0)),
        ]
        out_specs.append(pl.BlockSpec((COND_ROWS, N_ADA), lambda r: (0, 0)))
        out_shape.append(jax.ShapeDtypeStruct((COND_ROWS, N_ADA), F32))
        args += [cond_slabs, w_ada, b_ada.reshape(DEPTH, 1, N_ADA)]
    return pl.pallas_call(
        functools.partial(_attn_kernel, layer=layer, with_mod=with_mod),
        grid=(N_ATTN_STEPS,),
        in_specs=in_specs,
        out_specs=out_specs,
        out_shape=out_shape,
        compiler_params=_params("arbitrary"),
        name="swa_attention",
    )(*args)


def _merge_kernel(h_ref, yp_ref, ya_ref, wgp_ref, wga_ref, wbp_ref, wba_ref, o_ref,
                  sgp, sga, sbp, sba):
    j, i, slot = _ws_ids()

    def cast():
        _cast_chunk(wgp_ref, sgp, slot, i)
        _cast_chunk(wga_ref, sga, slot, i)
        _cast_chunk(wbp_ref, sbp, slot, i)
        _cast_chunk(wba_ref, sba, slot, i)

    @pl.when(j == 0)
    def _():
        cast()

    @pl.when(j > 0)
    def _():
        h = h_ref[...]
        gate_pool = jax.nn.sigmoid(_dot(h, sgp[1 - slot]))
        gate_attn = jax.nn.sigmoid(_dot(h, sga[1 - slot]))
        y_pool = _dot(yp_ref[...], sbp[1 - slot])
        y_attn = _dot(ya_ref[...], sba[1 - slot])
        o_ref[...] = (gate_pool * y_pool + gate_attn * y_attn).astype(o_ref.dtype)
        cast()


def _merge(h, yp, ya, w_gate, w_bp, w_ba, layer, tm=512, tn=512):
    ni, nj = ROWS // tm, D_MODEL // tn
    return pl.pallas_call(
        _merge_kernel,
        grid=(nj + 1, ni),
        in_specs=[
            _a_spec(tm, D_MODEL), _a_spec(tm, POOL_WIDTH), _a_spec(tm, Q_WIDTH),
            _w_spec(layer, D_MODEL, ni, tn, nj),
            _w_spec(layer, D_MODEL, ni, tn, nj, col_tile0=nj),
            _w_spec(layer, POOL_WIDTH, ni, tn, nj),
            _w_spec(layer, Q_WIDTH, ni, tn, nj),
        ],
        out_specs=_o_spec(tm, tn),
        out_shape=jax.ShapeDtypeStruct((ROWS, D_MODEL), BF16),
        scratch_shapes=[pltpu.VMEM((2, D_MODEL, tn), BF16), pltpu.VMEM((2, D_MODEL, tn), BF16),
                        pltpu.VMEM((2, POOL_WIDTH, tn), BF16), pltpu.VMEM((2, Q_WIDTH, tn), BF16)],
        compiler_params=_params("arbitrary", "arbitrary"),
        name="gated_merge",
    )(h, yp, ya, w_gate, w_gate, w_bp, w_ba)


def _resid_kernel(a_ref, w_ref, x_ref, g_ref, o_ref, scr):
    j, i, slot = _ws_ids()

    @pl.when(j == 0)
    def _():
        _cast_chunk(w_ref, scr, slot, i)

    @pl.when(j > 0)
    def _():
        o_ref[...] = x_ref[...] + g_ref[0] * _dot(a_ref[...], scr[1 - slot])
        _cast_chunk(w_ref, scr, slot, i)


def _resid_proj(a, w, x, gate, layer, tm, tn):
    m, k = a.shape
    n = x.shape[1]
    ni, nj = m // tm, n // tn
    per_batch = SEQ // tm
    return pl.pallas_call(
        _resid_kernel,
        grid=(nj + 1, ni),
        in_specs=[
            _a_spec(tm, k),
            _w_spec(layer, k, ni, tn, nj),
            _o_spec(tm, tn),
            pl.BlockSpec((1, 1, tn), lambda j, i: (i // per_batch, 0, jnp.maximum(j - 1, 0))),
        ],
        out_specs=_o_spec(tm, tn),
        out_shape=jax.ShapeDtypeStruct((m, n), F32),
        scratch_shapes=[pltpu.VMEM((2, k, tn), BF16)],
        compiler_params=_params("arbitrary", "arbitrary"),
        name="resid_proj",
    )(a, w, x, gate)


FF_TILE = V7X_MXU_DIM
SWIGLU_ROWS = 1024


def _swiglu_kernel(a_ref, wa_ref, wb_ref, o_ref, scr):
    j, i, slot = _ws_ids()

    def cast():
        _cast_chunk(wa_ref, scr, slot, i)
        _cast_chunk(wb_ref, scr, slot, i, col0=FF_TILE)

    @pl.when(j == 0)
    def _():
        cast()

    @pl.when(j > 0)
    def _():
        w = scr[1 - slot]
        for r0 in range(0, a_ref.shape[0], SWIGLU_ROWS):
            ab = _dot(a_ref[r0:r0 + SWIGLU_ROWS, :], w)
            a = ab[:, :FF_TILE]
            b = ab[:, FF_TILE:]
            o_ref[r0:r0 + SWIGLU_ROWS, :] = ((a * jax.nn.sigmoid(a)) * b).astype(o_ref.dtype)
        cast()


def _swiglu(h2, w_ffn_in, layer, tm=2048):
    ni, nj = ROWS // tm, D_FF // FF_TILE
    return pl.pallas_call(
        _swiglu_kernel,
        grid=(nj + 1, ni),
        in_specs=[
            _a_spec(tm, D_MODEL),
            _w_spec(layer, D_MODEL, ni, FF_TILE, nj),
            _w_spec(layer, D_MODEL, ni, FF_TILE, nj, col_tile0=nj),
        ],
        out_specs=_o_spec(tm, FF_TILE),
        out_shape=jax.ShapeDtypeStruct((ROWS, D_FF), BF16),
        scratch_shapes=[pltpu.VMEM((2, D_MODEL, 2 * FF_TILE), BF16)],
        compiler_params=_params("arbitrary", "arbitrary"),
        name="swiglu_up",
    )(h2, w_ffn_in, w_ffn_in)


def kernel(x, c, w_ada, b_ada, norm1, w_in, w_pool_mix, pool_scale, sinks, rel_bias,
           w_branch_pool, w_branch_attn, w_gate, w_out, norm2, w_ffn_in, w_ffn_out,
           final_norm):
    c_pad = jnp.pad(c, ((0, COND_ROWS - BATCH), (0, 0)))
    cond_slabs = c_pad.reshape(COND_ROWS, N_ATTN_STEPS, ADA_SIDE_ROWS).transpose(1, 0, 2)
    mod = _ada_first(c_pad, w_ada, b_ada)
    bias = _band_bias(rel_bias, _bucket_tables())
    xr = x.reshape(ROWS, D_MODEL)

    for l in range(DEPTH):
        sh1, sc1, g1, sh2, sc2, g2 = [m.reshape(BATCH, 1, D_MODEL)
                                      for m in jnp.split(mod[:BATCH], N_MOD, axis=-1)]
        h = _mod_norm(xr, norm1, l, sc1, sh1)
        u, q, kv = _proj(h, w_in, l)
        yp = _pool_branch(u, w_pool_mix, pool_scale, l)
        if l + 1 < DEPTH:
            ya, mod = _attention(q, kv, sinks, bias, l, cond_slabs, w_ada, b_ada)
        else:
            ya, = _attention(q, kv, sinks, bias, l)
        merged = _merge(h, yp, ya, w_gate, w_branch_pool, w_branch_attn, l)
        xr = _resid_proj(merged, w_out, xr, g1, l, tm=512, tn=1024)

        h2 = _mod_norm(xr, norm2, l, sc2, sh2)
        act = _swiglu(h2, w_ffn_in, l)
        xr = _resid_proj(act, w_ffn_out, xr, g2, l, tm=512, tn=512)

    return _final_norm(xr, final_norm).reshape(BATCH, SEQ, D_MODEL)
```
